```python
import math
import jax, jax.numpy as jnp
from jax import lax
import numpy as np

D_MODEL = 2048
BATCH = 8
SEQ = 4096
DEPTH = 4

EPS = 1e-6
D_FF = 5504
D_MIX = D_MODEL // 2
N_BRANCH = 3
A_CONV = 3
B_HEADS = 8
B_HEAD_DIM = D_MIX // B_HEADS
B_CONV = 4
B_CHUNK = 64
C_GROUPS = 8
C_CHUNK = 128
N_ADA = 9
N_NORMS = 6
IN_WIDTHS = (D_MIX, D_MIX, D_MIX,
             3 * D_MIX, D_MIX, B_HEADS, B_HEADS,
             D_MIX, D_MIX,
             D_MODEL, D_MODEL, D_MODEL)
N_IN = sum(IN_WIDTHS)
SPLIT_POINTS = tuple(int(s) for s in np.cumsum(IN_WIDTHS)[:-1])

kernel_name = 'hybrid_gated_parallel_mixer_trunk'


def rmsnorm(x, g):
    xf = x.astype(jnp.float32)
    y = xf * lax.rsqrt(jnp.mean(xf * xf, axis=-1, keepdims=True) + EPS)
    return (y * g.astype(jnp.float32)).astype(x.dtype)


def layernorm(x, g, b):
    xf = x.astype(jnp.float32)
    mu = jnp.mean(xf, axis=-1, keepdims=True)
    xc = xf - mu
    y = xc * lax.rsqrt(jnp.mean(xc * xc, axis=-1, keepdims=True) + EPS)
    return (y * g.astype(jnp.float32) + b.astype(jnp.float32)).astype(x.dtype)


def l2norm(x):
    xf = x.astype(jnp.float32)
    return (xf * lax.rsqrt(jnp.sum(xf * xf, axis=-1, keepdims=True) + EPS)).astype(x.dtype)


def causal_dwconv(x, w):
    K, C = w.shape
    return lax.conv_general_dilated(x, w[:, None, :], window_strides=(1,), padding=[(K - 1, 0)],
                                    dimension_numbers=('NWC', 'WIO', 'NWC'), feature_group_count=C)


def swiglu(n, w_gate, w_up, w_down):
    return (jax.nn.silu(n @ w_gate) * (n @ w_up)) @ w_down


def short_conv_mixer(xa, ba, ca, w_conv):
    return ba * causal_dwconv(ca * xa, w_conv)


def chunked_delta_rule(q, k, v, g, beta):
    Bsz, S, H, dk = q.shape
    dv = v.shape[-1]
    N, C = S // B_CHUNK, B_CHUNK
    f32 = jnp.float32
    chunks = lambda t: t.astype(f32).reshape(Bsz, N, C, H, -1).transpose(1, 0, 3, 2, 4)
    qc, kc, vc = chunks(q), chunks(k), chunks(v)
    gc = g.astype(f32).reshape(Bsz, N, C, H).transpose(1, 0, 3, 2)
    bc = beta.astype(f32).reshape(Bsz, N, C, H).transpose(1, 0, 3, 2)
    gcum = jnp.cumsum(gc, axis=-1)
    tri_incl = jnp.tril(jnp.ones((C, C), bool))
    tri_strict = jnp.tril(jnp.ones((C, C), bool), -1)
    decay = jnp.exp(jnp.where(tri_incl, gcum[..., :, None] - gcum[..., None, :], -jnp.inf))
    kb = kc * bc[..., None]
    m = jnp.where(tri_strict, jnp.einsum('nbhid,nbhjd->nbhij', kb, kc) * decay, 0.0)
    eye = jnp.eye(C, dtype=f32)
    rhs = jnp.concatenate([vc * bc[..., None], kb * jnp.exp(gcum)[..., None]], axis=-1)
    sol = lax.linalg.triangular_solve(eye + m, rhs, left_side=True, lower=True, unit_diagonal=True)
    u, w = sol[..., :dv], sol[..., dv:]
    attn = jnp.where(tri_incl, jnp.einsum('nbhid,nbhjd->nbhij', qc, kc) * decay, 0.0)
    qg = qc * jnp.exp(gcum)[..., None]
    kd = kc * jnp.exp(gcum[..., -1:] - gcum)[..., None]
    glast = jnp.exp(gcum[..., -1])

    def step(state, inp):
        qg_c, kd_c, u_c, w_c, a_c, gl_c = inp
        v_new = u_c - jnp.einsum('bhcd,bhde->bhce', w_c, state)
        o = jnp.einsum('bhcd,bhde->bhce', qg_c, state) + jnp.einsum('bhij,bhje->bhie', a_c, v_new)
        state = state * gl_c[..., None, None] + jnp.einsum('bhcd,bhce->bhde', kd_c, v_new)
        return state, o

    s0 = jnp.zeros((Bsz, H, dk, dv), f32)
    _, o = lax.scan(step, s0, (qg, kd, u, w, attn, glast))
    return o.transpose(1, 0, 3, 2, 4).reshape(Bsz, S, H, dv).astype(q.dtype)


def gated_deltanet(qkv, z, beta_logit, a_logit, w_conv, a_log, dt_bias, norm_g):
    Bsz, S, _ = qkv.shape
    qkv = jax.nn.silu(causal_dwconv(qkv, w_conv))
    q, k, v = jnp.split(qkv, 3, axis=-1)
    heads = lambda t: t.reshape(Bsz, S, B_HEADS, B_HEAD_DIM)
    q = l2norm(heads(q)) * (B_HEAD_DIM ** -0.5)
    k = l2norm(heads(k))
    beta = jax.nn.sigmoid(beta_logit.astype(jnp.float32))
    g = -jnp.exp(a_log.astype(jnp.float32)) * jax.nn.softplus(a_logit.astype(jnp.float32) + dt_bias.astype(jnp.float32))
    o = chunked_delta_rule(q, k, heads(v), g, beta)
    o = rmsnorm(o, norm_g) * jax.nn.silu(heads(z))
    return o.reshape(Bsz, S, D_MIX)


def chunk_spatial_gating(u, v, ln_g, ln_b, w_s, b_s):
    Bsz, S, _ = u.shape
    N = S // C_CHUNK
    u = jax.nn.gelu(u)
    v = layernorm(jax.nn.gelu(v), ln_g, ln_b)
    vg = v.reshape(Bsz, N, C_CHUNK, C_GROUPS, D_MIX // C_GROUPS)
    w_causal = jnp.where(jnp.tril(jnp.ones((C_CHUNK, C_CHUNK), bool)), w_s, 0.0)
    mixed = jnp.einsum('gij,bnjgc->bnigc', w_causal, vg) + b_s.T[:, :, None]
    return u * mixed.reshape(Bsz, S, D_MIX)


def hybrid_token_mixing(n, w_in, conv_a, conv_qkv, a_log, dt_bias, gdn_norm_g,
                        ln_v_g, ln_v_b, w_spatial, b_spatial, w_branch, w_o):
    proj = n @ w_in
    (xa, ba, ca, qkv, z, beta_logit, a_logit, uc, vc,
     g_a, g_b, g_c) = jnp.split(proj, SPLIT_POINTS, axis=-1)
    y_a = short_conv_mixer(xa, ba, ca, conv_a)
    y_b = gated_deltanet(qkv, z, beta_logit, a_logit, conv_qkv, a_log, dt_bias, gdn_norm_g)
    y_c = chunk_spatial_gating(uc, vc, ln_v_g, ln_v_b, w_spatial, b_spatial)
    merged = (jax.nn.sigmoid(g_a) * (y_a @ w_branch[0])
              + jax.nn.sigmoid(g_b) * (y_b @ w_branch[1])
              + jax.nn.sigmoid(g_c) * (y_c @ w_branch[2]))
    return merged @ w_o


def setup_inputs(seed: int = 0) -> dict:
    key = jax.random.key(seed)
    ks = jax.random.split(key, 22)
    f32 = jnp.float32
    L = DEPTH
    nrm = lambda k, shape, s: jax.random.normal(k, shape, f32) * s
    dt = jnp.exp(jax.random.uniform(ks[11], (L, B_HEADS), f32, math.log(1e-3), math.log(1e-1)))
    return {
        'x': nrm(ks[0], (BATCH, SEQ, D_MODEL), 1.0),
        'c': nrm(ks[1], (BATCH, D_MODEL), 1.0),
        'w_ada': nrm(ks[2], (L, D_MODEL, N_ADA * D_MODEL), 0.5 * D_MODEL ** -0.5),
        'b_ada': nrm(ks[3], (L, N_ADA * D_MODEL), 0.01),
        'norm_g': 1.0 + nrm(ks[4], (L, N_NORMS, D_MODEL), 0.05),
        'ffn_w_gate': nrm(ks[5], (L, 2, D_MODEL, D_FF), D_MODEL ** -0.5),
        'ffn_w_up': nrm(ks[6], (L, 2, D_MODEL, D_FF), D_MODEL ** -0.5),
        'ffn_w_down': nrm(ks[7], (L, 2, D_FF, D_MODEL), D_FF ** -0.5),
        'w_in': nrm(ks[8], (L, D_MODEL, N_IN), D_MODEL ** -0.5),
        'conv_a': nrm(ks[9], (L, A_CONV, D_MIX), A_CONV ** -0.5),
        'conv_qkv': nrm(ks[10], (L, B_CONV, 3 * D_MIX), B_CONV ** -0.5),
        'a_log': jnp.log(jax.random.uniform(ks[12], (L, B_HEADS), f32, 1.0, 16.0)),
        'dt_bias': dt + jnp.log(-jnp.expm1(-dt)),
        'gdn_norm_g': 1.0 + nrm(ks[13], (L, B_HEAD_DIM), 0.05),
        'ln_v_g': 1.0 + nrm(ks[14], (L, D_MIX), 0.05),
        'ln_v_b': nrm(ks[15], (L, D_MIX), 0.01),
        'w_spatial': nrm(ks[16], (L, C_GROUPS, C_CHUNK, C_CHUNK), C_CHUNK ** -0.5),
        'b_spatial': 1.0 + nrm(ks[17], (L, C_GROUPS, C_CHUNK), 0.01),
        'w_branch': nrm(ks[18], (L, N_BRANCH, D_MIX, D_MODEL), D_MIX ** -0.5),
        'w_o': nrm(ks[19], (L, D_MODEL, D_MODEL), D_MODEL ** -0.5),
    }


def reference(x, c, w_ada, b_ada, norm_g, ffn_w_gate, ffn_w_up, ffn_w_down, w_in, conv_a, conv_qkv,
              a_log, dt_bias, gdn_norm_g, ln_v_g, ln_v_b, w_spatial, b_spatial, w_branch, w_o):
    c_act = jax.nn.silu(c)
    h = x
    for layer in range(DEPTH):
        ada = (c_act @ w_ada[layer] + b_ada[layer])[:, None, :]
        sh1, sc1, gt1, sh2, sc2, gt2, sh3, sc3, gt3 = jnp.split(ada, N_ADA, axis=-1)
        ng = norm_g[layer]
        n = rmsnorm(h, ng[0]) * (1.0 + sc1) + sh1
        f = swiglu(n, ffn_w_gate[layer, 0], ffn_w_up[layer, 0], ffn_w_down[layer, 0])
        h = h + 0.5 * gt1 * rmsnorm(f, ng[1])
        n = rmsnorm(h, ng[2]) * (1.0 + sc2) + sh2
        y = hybrid_token_mixing(n, w_in[layer], conv_a[layer], conv_qkv[layer], a_log[layer], dt_bias[layer],
                                gdn_norm_g[layer], ln_v_g[layer], ln_v_b[layer], w_spatial[layer],
                                b_spatial[layer], w_branch[layer], w_o[layer])
        h = h + gt2 * rmsnorm(y, ng[3])
        n = rmsnorm(h, ng[4]) * (1.0 + sc3) + sh3
        f = swiglu(n, ffn_w_gate[layer, 1], ffn_w_up[layer, 1], ffn_w_down[layer, 1])
        h = h + 0.5 * gt3 * rmsnorm(f, ng[5])
    return h
```

```python
import functools

import jax
import jax.numpy as jnp
from jax import lax
from jax.experimental import pallas as pl
from jax.experimental.pallas import tpu as pltpu

F32 = jnp.float32
BF16 = jnp.bfloat16

EPS = 1e-6
HEADS = 8
HEAD_DIM = 128
DELTA_CHUNK = 64
GROUPS = 8
SPATIAL_CHUNK = 128
N_ADA = 9
HALO = 16
SMALL = 128

VMEM_LIMIT = 56 * 1024 * 1024

FFN_TM, FFN_TF = 512, 512
PROJ_TM, PROJ_TN = 512, 512
PREP_T = 256
DELTA_T = 256
MIX_T = 256
ADA_TN = 1024


def _sigmoid(x):
    return 1.0 / (1.0 + jnp.exp(-x))


def _silu(x):
    return x * _sigmoid(x)


def _gelu_tanh(x):
    return 0.5 * x * (1.0 + jnp.tanh(0.7978845608028654 * (x + 0.044715 * (x * x * x))))


def _softplus(x):
    return jnp.maximum(x, 0.0) + jnp.log(1.0 + jnp.exp(-jnp.abs(x)))


def _rms(x):
    return x * lax.rsqrt(jnp.mean(x * x, axis=-1, keepdims=True) + EPS)


def _dot(a, b):
    return jnp.dot(a, b, preferred_element_type=F32)


def _params(*sem):
    return pltpu.CompilerParams(dimension_semantics=sem, vmem_limit_bytes=VMEM_LIMIT)


def _ada_kernel(c_ref, w_ref, b_ref, o_ref):
    c = c_ref[...]
    act = _silu(c).astype(BF16)
    o_ref[0] = _dot(act, w_ref[0].astype(BF16)) + b_ref[0]


def _ada_call(c, w_ada, b_ada):
    L, D, N = w_ada.shape
    B = c.shape[0]
    return pl.pallas_call(
        _ada_kernel,
        grid=(L, N // ADA_TN),
        in_specs=[
            pl.BlockSpec((B, D), lambda l, j: (0, 0)),
            pl.BlockSpec((1, D, ADA_TN), lambda l, j: (l, 0, j)),
            pl.BlockSpec((1, 1, ADA_TN), lambda l, j: (l, 0, j)),
        ],
        out_specs=pl.BlockSpec((1, B, ADA_TN), lambda l, j: (l, 0, j)),
        out_shape=jax.ShapeDtypeStruct((L, B, N), F32),
        compiler_params=_params("parallel", "parallel"),
        name="ada",
    )(c, w_ada, b_ada.reshape(L, 1, N))


def _ffn_kernel(h_ref, sh_ref, sc_ref, gt_ref, gpre_ref, gpost_ref, wg_ref, wu_ref, wd_ref,
                o_ref, n_scr, acc_scr):
    j = pl.program_id(2)

    @pl.when(j == 0)
    def _():
        n = _rms(h_ref[0]) * gpre_ref[0] * (1.0 + sc_ref[0]) + sh_ref[0]
        n_scr[...] = n.astype(BF16)

    n = n_scr[...]
    gate = _dot(n, wg_ref[...])
    up = _dot(n, wu_ref[...])
    hid = (_silu(gate) * up).astype(BF16)
    part = _dot(hid, wd_ref[...])

    @pl.when(j == 0)
    def _():
        acc_scr[...] = part

    @pl.when(j > 0)
    def _():
        acc_scr[...] += part

    @pl.when(j == pl.num_programs(2) - 1)
    def _():
        r = _rms(acc_scr[...]) * gpost_ref[0]
        o_ref[0] = h_ref[0] + (0.5 * gt_ref[0]) * r


def _ffn_call(h, ada, ng, k0, wg, wu, wd):
    B, S, D = h.shape
    Fp = wg.shape[1]
    tm, tf = min(FFN_TM, S), FFN_TF
    ada_spec = lambda k: pl.BlockSpec((1, 1, D), lambda b, i, j: (b * N_ADA + k, 0, 0))
    ng_spec = lambda k: pl.BlockSpec((1, 1, D), lambda b, i, j: (k, 0, 0))
    npre = (k0 // 3) * 2
    return pl.pallas_call(
        _ffn_kernel,
        grid=(B, S // tm, Fp // tf),
        in_specs=[
            pl.BlockSpec((1, tm, D), lambda b, i, j: (b, i, 0)),
            ada_spec(k0), ada_spec(k0 + 1), ada_spec(k0 + 2),
            ng_spec(npre), ng_spec(npre + 1),
            pl.BlockSpec((D, tf), lambda b, i, j: (0, j)),
            pl.BlockSpec((D, tf), lambda b, i, j: (0, j)),
            pl.BlockSpec((tf, D), lambda b, i, j: (j, 0)),
        ],
        out_specs=pl.BlockSpec((1, tm, D), lambda b, i, j: (b, i, 0)),
        out_shape=jax.ShapeDtypeStruct((B, S, D), F32),
        scratch_shapes=[pltpu.VMEM((tm, D), BF16), pltpu.VMEM((tm, D), F32)],
        compiler_params=_params("parallel", "parallel", "arbitrary"),
        name="ffn",
    )(h, ada, ada, ada, ng, ng, wg, wu, wd)


def _proj_kernel(h_ref, sh_ref, sc_ref, g_ref, w_ref, ws_ref, o_ref, os_ref, n_scr):
    j = pl.program_id(2)

    @pl.when(j == 0)
    def _():
        n = (_rms(h_ref[0]) * g_ref[0] * (1.0 + sc_ref[0]) + sh_ref[0]).astype(BF16)
        n_scr[...] = n
        os_ref[0] = _dot(n, ws_ref[...])

    o_ref[0] = _dot(n_scr[...], w_ref[...]).astype(BF16)


def _proj_call(h, ada, ng, w_main, w_small):
    B, S, D = h.shape
    N = w_main.shape[1]
    tm, tn = min(PROJ_TM, S), PROJ_TN
    ada_spec = lambda k: pl.BlockSpec((1, 1, D), lambda b, i, j: (b * N_ADA + k, 0, 0))
    return pl.pallas_call(
        _proj_kernel,
        grid=(B, S // tm, N // tn),
        in_specs=[
            pl.BlockSpec((1, tm, D), lambda b, i, j: (b, i, 0)),
            ada_spec(3), ada_spec(4),
            pl.BlockSpec((1, 1, D), lambda b, i, j: (2, 0, 0)),
            pl.BlockSpec((D, tn), lambda b, i, j: (0, j)),
            pl.BlockSpec((D, SMALL), lambda b, i, j: (0, 0)),
        ],
        out_specs=[
            pl.BlockSpec((1, tm, tn), lambda b, i, j: (b, i, j)),
            pl.BlockSpec((1, tm, SMALL), lambda b, i, j: (b, i, 0)),
        ],
        out_shape=[jax.ShapeDtypeStruct((B, S, N), BF16), jax.ShapeDtypeStruct((B, S, SMALL), F32)],
        scratch_shapes=[pltpu.VMEM((tm, D), BF16)],
        compiler_params=_params("parallel", "parallel", "arbitrary"),
        name="proj",
    )(h, ada, ada, ng, w_main, w_small)


def _proj_layout(D):
    dm = D // 2
    gates = 0
    qkv = 3 * D
    rest = qkv + 3 * dm
    blk = rest // dm
    return dict(xa=blk, ba=blk + 1, ca=blk + 2, z=blk + 3, uc=blk + 4, vc=blk + 5)


def _prep_kernel(x_ref, halo_ref, w_ref, o_ref, ext_scr):
    t = pl.program_id(1)
    T = x_ref.shape[1]
    dm = x_ref.shape[2] // 3
    x = x_ref[0].astype(F32)
    halo = jnp.where(t > 0, halo_ref[0].astype(F32), 0.0)
    ext_scr[0:HALO, :] = halo
    ext_scr[HALO:HALO + T, :] = x
    K = w_ref.shape[0]
    acc = w_ref[K - 1:K, :] * x
    for s in range(1, K):
        acc = acc + w_ref[K - 1 - s:K - s, :] * ext_scr[HALO - s:HALO - s + T, :]
    y = _silu(acc)
    for i in range(2 * HEADS):
        sl = slice(i * HEAD_DIM, (i + 1) * HEAD_DIM)
        v = y[:, sl]
        v = v * lax.rsqrt(jnp.sum(v * v, axis=-1, keepdims=True) + EPS)
        if i < HEADS:
            v = v * (HEAD_DIM ** -0.5)
        o_ref[0, :, sl] = v.astype(BF16)
    o_ref[0, :, 2 * dm:3 * dm] = y[:, 2 * dm:3 * dm].astype(BF16)


def _prep_call(proj, conv_w):
    B, S, _ = proj.shape
    K, W = conv_w.shape
    T = min(PREP_T, S)
    rb = T // HALO
    return pl.pallas_call(
        _prep_kernel,
        grid=(B, S // T),
        in_specs=[
            pl.BlockSpec((1, T, W), lambda b, t: (b, t, 2)),
            pl.BlockSpec((1, HALO, W), lambda b, t: (b, jnp.maximum(t * rb - 1, 0), 2)),
            pl.BlockSpec((K, W), lambda b, t: (0, 0)),
        ],
        out_specs=pl.BlockSpec((1, T, W), lambda b, t: (b, t, 0)),
        out_shape=jax.ShapeDtypeStruct((B, S, W), BF16),
        scratch_shapes=[pltpu.VMEM((HALO + T, W), F32)],
        compiler_params=_params("parallel", "parallel"),
        name="qkv_prep",
    )(proj, proj, conv_w)


def _delta_kernel(qkv_ref, z_ref, sm_ref, alog_ref, dt_ref, ng_ref, o_ref, s_scr):
    t = pl.program_id(1)
    T = qkv_ref.shape[1]
    dm = HEADS * HEAD_DIM
    C = DELTA_CHUNK
    nchunk = T // C

    @pl.when(t == 0)
    def _():
        s_scr[...] = jnp.zeros_like(s_scr)

    sm = sm_ref[0]
    beta_all = _sigmoid(sm)
    g_all = -jnp.exp(alog_ref[...]) * _softplus(sm + dt_ref[...])

    ri = lax.broadcasted_iota(jnp.int32, (T, T), 0)
    ci = lax.broadcasted_iota(jnp.int32, (T, T), 1)
    shift = C.bit_length() - 1
    same = jnp.right_shift(ri, shift) == jnp.right_shift(ci, shift)
    incl = same & (ri >= ci)
    strict = same & (ri > ci)
    tri = jnp.where(incl, 1.0, 0.0).astype(BF16)

    g_hi = g_all.astype(BF16)
    r1 = g_all - g_hi.astype(F32)
    g_mid = r1.astype(BF16)
    g_lo = (r1 - g_mid.astype(F32)).astype(BF16)
    gcum = _dot(tri, g_hi) + _dot(tri, g_mid) + _dot(tri, g_lo)
    glast = jnp.concatenate(
        [jnp.broadcast_to(gcum[c * C + C - 1:c * C + C, :], (C, SMALL)) for c in range(nchunk)], axis=0)
    eg_all = jnp.exp(gcum)
    kdec_all = jnp.exp(glast - gcum)
    egl_all = jnp.exp(glast)
    gcum_t = gcum.T

    for h in range(HEADS):
        hs = slice(h * HEAD_DIM, (h + 1) * HEAD_DIM)
        q = qkv_ref[0, :, hs]
        k = qkv_ref[0, :, dm + h * HEAD_DIM:dm + (h + 1) * HEAD_DIM]
        v = qkv_ref[0, :, 2 * dm + h * HEAD_DIM:2 * dm + (h + 1) * HEAD_DIM]
        qf, kf, vf = q.astype(F32), k.astype(F32), v.astype(F32)
        beta = beta_all[:, h:h + 1]
        gc = gcum[:, HEADS + h:HEADS + h + 1]
        gr = gcum_t[HEADS + h:HEADS + h + 1, :]
        eg = eg_all[:, HEADS + h:HEADS + h + 1]
        kdec = kdec_all[:, HEADS + h:HEADS + h + 1]

        dec = jnp.exp(jnp.where(incl, gc - gr, -jnp.inf))
        kb = kf * beta
        lhs = jnp.concatenate([kb, qf], axis=0).astype(BF16)
        sc = lax.dot_general(lhs, k, (((1,), (1,)), ((), ())), preferred_element_type=F32)
        m = jnp.where(strict, sc[:T] * dec, 0.0)
        attn = jnp.where(incl, sc[T:] * dec, 0.0)

        x = jnp.concatenate([vf * beta, kb * eg], axis=1)
        pb = m.astype(BF16)
        x = x - _dot(pb, x.astype(BF16))
        lvl = 2
        while lvl < C:
            pb = _dot(pb, pb).astype(BF16)
            x = x + _dot(pb, x.astype(BF16))
            lvl *= 2
        u, w = x[:, :HEAD_DIM], x[:, HEAD_DIM:]

        qg = qf * eg
        kd_t = (kf * kdec).T
        state = s_scr[h]
        outs = []
        for c in range(nchunk):
            rows = slice(c * C, (c + 1) * C)
            a = jnp.concatenate([w[rows], qg[rows]], axis=0).astype(BF16)
            r = _dot(a, state.astype(BF16))
            v_new = u[rows] - r[:C]
            pieces = []
            if c > 0:
                pieces.append(jnp.zeros((c * C, HEAD_DIM), F32))
            pieces.append(v_new)
            if c < nchunk - 1:
                pieces.append(jnp.zeros((T - (c + 1) * C, HEAD_DIM), F32))
            v_pad = jnp.concatenate(pieces, axis=0).astype(BF16)
            lb = jnp.concatenate([attn[rows], kd_t], axis=0).astype(BF16)
            r2 = _dot(lb, v_pad)
            outs.append(r[C:] + r2[:C])
            gl = egl_all[c * C:c * C + 1, HEADS + h:HEADS + h + 1]
            state = state * gl + r2[C:]
        s_scr[h] = state
        o = jnp.concatenate(outs, axis=0)
        zf = z_ref[0, :, hs].astype(F32)
        o_ref[0, :, hs] = (_rms(o) * ng_ref[...] * _silu(zf)).astype(BF16)


def _delta_call(qkv, proj, small, alog_row, dt_row, gdn_g, z_blk):
    B, S, W = qkv.shape
    dm = W // 3
    T = min(DELTA_T, S)
    return pl.pallas_call(
        _delta_kernel,
        grid=(B, S // T),
        in_specs=[
            pl.BlockSpec((1, T, W), lambda b, t: (b, t, 0)),
            pl.BlockSpec((1, T, dm), lambda b, t: (b, t, z_blk)),
            pl.BlockSpec((1, T, SMALL), lambda b, t: (b, t, 0)),
            pl.BlockSpec((1, SMALL), lambda b, t: (0, 0)),
            pl.BlockSpec((1, SMALL), lambda b, t: (0, 0)),
            pl.BlockSpec((1, HEAD_DIM), lambda b, t: (0, 0)),
        ],
        out_specs=pl.BlockSpec((1, T, dm), lambda b, t: (b, t, 0)),
        out_shape=jax.ShapeDtypeStruct((B, S, dm), BF16),
        scratch_shapes=[pltpu.VMEM((HEADS, HEAD_DIM, HEAD_DIM), F32)],
        compiler_params=_params("parallel", "arbitrary"),
        name="delta",
    )(qkv, proj, small, alog_row, dt_row, gdn_g)


def _mix_kernel(h_ref, gates_ref, xa_ref, ba_ref, ca_ref, xah_ref, cah_ref, uc_ref, vc_ref, yb_ref,
                gt_ref, ng_ref, cw_ref, lng_ref, lnb_ref, wsp_ref, bsp_ref, wb_ref, wo_ref,
                o_ref, ext_scr, yc_scr):
    t = pl.program_id(1)
    T = h_ref.shape[1]
    D = h_ref.shape[2]
    SC = SPATIAL_CHUNK
    nchunk = T // SC

    p = ca_ref[0].astype(F32) * xa_ref[0].astype(F32)
    ph = jnp.where(t > 0, cah_ref[0].astype(F32) * xah_ref[0].astype(F32), 0.0)
    ext_scr[0:HALO, :] = ph
    ext_scr[HALO:HALO + T, :] = p
    K = cw_ref.shape[0]
    conv = cw_ref[K - 1:K, :] * p
    for s in range(1, K):
        conv = conv + cw_ref[K - 1 - s:K - s, :] * ext_scr[HALO - s:HALO - s + T, :]
    y_a = (ba_ref[0].astype(F32) * conv).astype(BF16)

    v = _gelu_tanh(vc_ref[0].astype(F32))
    vcen = v - jnp.mean(v, axis=-1, keepdims=True)
    vn = vcen * lax.rsqrt(jnp.mean(vcen * vcen, axis=-1, keepdims=True) + EPS)
    vn = (vn * lng_ref[...] + lnb_ref[...]).astype(BF16)
    ri = lax.broadcasted_iota(jnp.int32, (SC, SC), 0)
    ci = lax.broadcasted_iota(jnp.int32, (SC, SC), 1)
    causal = ri >= ci
    gw = SC
    for g in range(GROUPS):
        gs = slice(g * gw, (g + 1) * gw)
        wg = jnp.where(causal, wsp_ref[g], 0.0).astype(BF16)
        rhs = jnp.concatenate([vn[c * SC:(c + 1) * SC, gs] for c in range(nchunk)], axis=1)
        mx = _dot(wg, rhs)
        mixed = jnp.concatenate([mx[:, c * gw:(c + 1) * gw] for c in range(nchunk)], axis=0)
        bias = jnp.concatenate([bsp_ref[:, gs]] * nchunk, axis=0)
        u = _gelu_tanh(uc_ref[0, :, gs].astype(F32))
        yc_scr[:, gs] = (u * (mixed + bias)).astype(BF16)

    gates = gates_ref[0]
    merged = _sigmoid(gates[:, 0:D].astype(F32)) * _dot(y_a, wb_ref[0])
    merged = merged + _sigmoid(gates[:, D:2 * D].astype(F32)) * _dot(yb_ref[0], wb_ref[1])
    merged = merged + _sigmoid(gates[:, 2 * D:3 * D].astype(F32)) * _dot(yc_scr[...], wb_ref[2])
    y = _dot(merged.astype(BF16), wo_ref[...])
    o_ref[0] = h_ref[0] + gt_ref[0] * (_rms(y) * ng_ref[0])


def _mix_call(h, proj, yb, ada, ng, conv_a, ln_g, ln_b, w_sp, b_sp, wb, wo):
    B, S, D = h.shape
    dm = D // 2
    lay = _proj_layout(D)
    T = min(MIX_T, S)
    rb = T // HALO
    K = conv_a.shape[0]
    one = pl.Buffered(1)
    tile = lambda blk: pl.BlockSpec((1, T, dm), lambda b, t: (b, t, blk))
    halo = lambda blk: pl.BlockSpec((1, HALO, dm), lambda b, t: (b, jnp.maximum(t * rb - 1, 0), blk))
    const = lambda shape: pl.BlockSpec(shape, lambda b, t: (0,) * len(shape), pipeline_mode=one)
    return pl.pallas_call(
        _mix_kernel,
        grid=(B, S // T),
        in_specs=[
            pl.BlockSpec((1, T, D), lambda b, t: (b, t, 0)),
            pl.BlockSpec((1, T, 3 * D), lambda b, t: (b, t, 0)),
            tile(lay["xa"]), tile(lay["ba"]), tile(lay["ca"]),
            halo(lay["xa"]), halo(lay["ca"]),
            tile(lay["uc"]), tile(lay["vc"]),
            pl.BlockSpec((1, T, dm), lambda b, t: (b, t, 0)),
            pl.BlockSpec((1, 1, D), lambda b, t: (b * N_ADA + 5, 0, 0)),
            pl.BlockSpec((1, 1, D), lambda b, t: (3, 0, 0)),
            const((K, dm)), const((1, dm)), const((1, dm)),
            const((GROUPS, SPATIAL_CHUNK, SPATIAL_CHUNK)), const((SPATIAL_CHUNK, dm)),
            const((3, dm, D)), const((D, D)),
        ],
        out_specs=pl.BlockSpec((1, T, D), lambda b, t: (b, t, 0)),
        out_shape=jax.ShapeDtypeStruct((B, S, D), F32),
        scratch_shapes=[pltpu.VMEM((HALO + T, dm), F32), pltpu.VMEM((T, dm), BF16)],
        compiler_params=_params("parallel", "parallel"),
        name="mix",
    )(h, proj, proj, proj, proj, proj, proj, proj, proj, yb, ada, ng,
      conv_a, ln_g, ln_b, w_sp, b_sp, wb, wo)


def _pad_axis(w, axis, mult):
    pad = (-w.shape[axis]) % mult
    widths = [(0, pad) if a == axis else (0, 0) for a in range(w.ndim)]
    return jnp.pad(w, widths) if pad else w


def kernel(x, c, w_ada, b_ada, norm_g, ffn_w_gate, ffn_w_up, ffn_w_down, w_in, conv_a, conv_qkv, a_log,
           dt_bias, gdn_norm_g, ln_v_g, ln_v_b, w_spatial, b_spatial, w_branch, w_o):
    B, S, D = x.shape
    L = w_ada.shape[0]
    dm = D // 2
    assert dm == HEADS * HEAD_DIM and dm == GROUPS * SPATIAL_CHUNK
    lay = _proj_layout(D)

    ada_all = _ada_call(c, w_ada, b_ada)

    o_qkv, o_z, o_small = 3 * dm, 6 * dm, 7 * dm
    o_uc = o_small + 2 * HEADS
    o_g = o_uc + 2 * dm

    h = x
    for l in range(L):
        ada = ada_all[l].reshape(B * N_ADA, 1, D)
        ng = norm_g[l].reshape(-1, 1, D)

        def ffn_weights(s):
            wg = _pad_axis(ffn_w_gate[l, s].astype(BF16), 1, FFN_TF)
            wu = _pad_axis(ffn_w_up[l, s].astype(BF16), 1, FFN_TF)
            wd = _pad_axis(ffn_w_down[l, s].astype(BF16), 0, FFN_TF)
            return wg, wu, wd

        h = _ffn_call(h, ada, ng, 0, *ffn_weights(0))

        wi = w_in[l]
        w_main = jnp.concatenate(
            [wi[:, o_g:], wi[:, o_qkv:o_z], wi[:, :o_qkv], wi[:, o_z:o_small], wi[:, o_uc:o_g]], axis=1).astype(BF16)
        w_small = _pad_axis(wi[:, o_small:o_uc], 1, SMALL).astype(BF16)
        proj, small = _proj_call(h, ada, ng, w_main, w_small)

        qkv = _prep_call(proj, conv_qkv[l])
        lane_pad = lambda v: jnp.pad(v, (HEADS, SMALL - 2 * HEADS)).reshape(1, SMALL)
        yb = _delta_call(qkv, proj, small, lane_pad(a_log[l]), lane_pad(dt_bias[l]),
                         gdn_norm_g[l].reshape(1, HEAD_DIM), lay["z"])

        b_exp = jnp.repeat(b_spatial[l].T, dm // GROUPS, axis=1)
        h = _mix_call(h, proj, yb, ada, ng, conv_a[l], ln_v_g[l].reshape(1, dm), ln_v_b[l].reshape(1, dm),
                      w_spatial[l], b_exp, w_branch[l].astype(BF16), w_o[l].astype(BF16))

        h = _ffn_call(h, ada, ng, 6, *ffn_weights(1))
    return h
```

```python
import functools

import jax
import jax.numpy as jnp
from jax import lax
from jax.experimental import pallas as pl
from jax.experimental.pallas import tpu as pltpu

F32 = jnp.float32
BF16 = jnp.bfloat16

EPS = 1e-6
HEADS = 8
HEAD_DIM = 128
DELTA_CHUNK = 64
GROUPS = 8
SPATIAL_CHUNK = 128
N_ADA = 9
HALO = 16
SMALL = 128

VMEM_LIMIT = 56 * 1024 * 1024

FFN_TM, FFN_TF = 1024, 256
PROJ_TM, PROJ_TN = 1024, 512
PREP_T = 256
DELTA_T = 256
MIX_T = 256
ADA_TN = 1024


def _sigmoid(x):
    return 1.0 / (1.0 + jnp.exp(-x))


def _silu(x):
    return x * _sigmoid(x)


def _gelu_tanh(x):
    return 0.5 * x * (1.0 + jnp.tanh(0.7978845608028654 * (x + 0.044715 * (x * x * x))))


def _softplus(x):
    return jnp.maximum(x, 0.0) + jnp.log(1.0 + jnp.exp(-jnp.abs(x)))


def _rms(x):
    return x * lax.rsqrt(jnp.mean(x * x, axis=-1, keepdims=True) + EPS)


def _dot(a, b):
    return jnp.dot(a, b, preferred_element_type=F32)


def _params(*sem):
    return pltpu.CompilerParams(dimension_semantics=sem, vmem_limit_bytes=VMEM_LIMIT)


def _ada_kernel(c_ref, w_ref, b_ref, o_ref):
    c = c_ref[...]
    act = _silu(c).astype(BF16)
    o_ref[0] = _dot(act, w_ref[0].astype(BF16)) + b_ref[0]


def _ada_call(c, w_ada, b_ada):
    L, D, N = w_ada.shape
    B = c.shape[0]
    return pl.pallas_call(
        _ada_kernel,
        grid=(L, N // ADA_TN),
        in_specs=[
            pl.BlockSpec((B, D), lambda l, j: (0, 0)),
            pl.BlockSpec((1, D, ADA_TN), lambda l, j: (l, 0, j)),
            pl.BlockSpec((1, 1, ADA_TN), lambda l, j: (l, 0, j)),
        ],
        out_specs=pl.BlockSpec((1, B, ADA_TN), lambda l, j: (l, 0, j)),
        out_shape=jax.ShapeDtypeStruct((L, B, N), F32),
        compiler_params=_params("parallel", "parallel"),
        name="ada",
    )(c, w_ada, b_ada.reshape(L, 1, N))


def _ffn_kernel(h_ref, sh_ref, sc_ref, gt_ref, gpre_ref, gpost_ref, wg_ref, wu_ref, wd_ref,
                o_ref, n_scr):
    j = pl.program_id(2)

    @pl.when(j == 0)
    def _():
        n = _rms(h_ref[0]) * gpre_ref[0] * (1.0 + sc_ref[0]) + sh_ref[0]
        n_scr[...] = n.astype(BF16)
        o_ref[...] = jnp.zeros_like(o_ref)

    n = n_scr[...]
    gate = _dot(n, wg_ref[...])
    up = _dot(n, wu_ref[...])
    hid = (_silu(gate) * up).astype(BF16)
    o_ref[0] += _dot(hid, wd_ref[...])

    @pl.when(j == pl.num_programs(2) - 1)
    def _():
        r = _rms(o_ref[0]) * gpost_ref[0]
        o_ref[0] = h_ref[0] + (0.5 * gt_ref[0]) * r


def _ffn_call(h, ada, ng, k0, wg, wu, wd):
    B, S, D = h.shape
    Fp = wg.shape[1]
    tm, tf = min(FFN_TM, S), FFN_TF
    ada_spec = lambda k: pl.BlockSpec((1, 1, D), lambda b, i, j: (b * N_ADA + k, 0, 0))
    ng_spec = lambda k: pl.BlockSpec((1, 1, D), lambda b, i, j: (k, 0, 0))
    npre = (k0 // 3) * 2
    return pl.pallas_call(
        _ffn_kernel,
        grid=(B, S // tm, Fp // tf),
        in_specs=[
            pl.BlockSpec((1, tm, D), lambda b, i, j: (b, i, 0)),
            ada_spec(k0), ada_spec(k0 + 1), ada_spec(k0 + 2),
            ng_spec(npre), ng_spec(npre + 1),
            pl.BlockSpec((D, tf), lambda b, i, j: (0, j)),
            pl.BlockSpec((D, tf), lambda b, i, j: (0, j)),
            pl.BlockSpec((tf, D), lambda b, i, j: (j, 0)),
        ],
        out_specs=pl.BlockSpec((1, tm, D), lambda b, i, j: (b, i, 0)),
        out_shape=jax.ShapeDtypeStruct((B, S, D), F32),
        scratch_shapes=[pltpu.VMEM((tm, D), BF16)],
        compiler_params=_params("parallel", "parallel", "arbitrary"),
        name="ffn",
    )(h, ada, ada, ada, ng, ng, wg, wu, wd)


def _proj_kernel(h_ref, sh_ref, sc_ref, g_ref, w_ref, ws_ref, o_ref, os_ref, n_scr):
    j = pl.program_id(2)

    @pl.when(j == 0)
    def _():
        n = (_rms(h_ref[0]) * g_ref[0] * (1.0 + sc_ref[0]) + sh_ref[0]).astype(BF16)
        n_scr[...] = n
        os_ref[0] = _dot(n, ws_ref[...])

    o_ref[0] = _dot(n_scr[...], w_ref[...]).astype(BF16)


def _proj_call(h, ada, ng, w_main, w_small):
    B, S, D = h.shape
    N = w_main.shape[1]
    tm, tn = min(PROJ_TM, S), PROJ_TN
    ada_spec = lambda k: pl.BlockSpec((1, 1, D), lambda b, i, j: (b * N_ADA + k, 0, 0))
    return pl.pallas_call(
        _proj_kernel,
        grid=(B, S // tm, N // tn),
        in_specs=[
            pl.BlockSpec((1, tm, D), lambda b, i, j: (b, i, 0)),
            ada_spec(3), ada_spec(4),
            pl.BlockSpec((1, 1, D), lambda b, i, j: (2, 0, 0)),
            pl.BlockSpec((D, tn), lambda b, i, j: (0, j)),
            pl.BlockSpec((D, SMALL), lambda b, i, j: (0, 0)),
        ],
        out_specs=[
            pl.BlockSpec((1, tm, tn), lambda b, i, j: (b, i, j)),
            pl.BlockSpec((1, tm, SMALL), lambda b, i, j: (b, i, 0)),
        ],
        out_shape=[jax.ShapeDtypeStruct((B, S, N), BF16), jax.ShapeDtypeStruct((B, S, SMALL), F32)],
        scratch_shapes=[pltpu.VMEM((tm, D), BF16)],
        compiler_params=_params("parallel", "parallel", "arbitrary"),
        name="proj",
    )(h, ada, ada, ng, w_main, w_small)


def _proj_layout(D):
    dm = D // 2
    gates = 0
    qkv = 3 * D
    rest = qkv + 3 * dm
    blk = rest // dm
    return dict(xa=blk, ba=blk + 1, ca=blk + 2, z=blk + 3, uc=blk + 4, vc=blk + 5)


def _prep_kernel(x_ref, halo_ref, w_ref, o_ref, ext_scr):
    t = pl.program_id(1)
    T = x_ref.shape[1]
    dm = x_ref.shape[2] // 3
    x = x_ref[0].astype(F32)
    halo = jnp.where(t > 0, halo_ref[0].astype(F32), 0.0)
    ext_scr[0:HALO, :] = halo
    ext_scr[HALO:HALO + T, :] = x
    K = w_ref.shape[0]
    acc = w_ref[K - 1:K, :] * x
    for s in range(1, K):
        acc = acc + w_ref[K - 1 - s:K - s, :] * ext_scr[HALO - s:HALO - s + T, :]
    y = _silu(acc)
    for i in range(2 * HEADS):
        sl = slice(i * HEAD_DIM, (i + 1) * HEAD_DIM)
        v = y[:, sl]
        v = v * lax.rsqrt(jnp.sum(v * v, axis=-1, keepdims=True) + EPS)
        if i < HEADS:
            v = v * (HEAD_DIM ** -0.5)
        o_ref[0, :, sl] = v.astype(BF16)
    o_ref[0, :, 2 * dm:3 * dm] = y[:, 2 * dm:3 * dm].astype(BF16)


def _prep_call(proj, conv_w):
    B, S, _ = proj.shape
    K, W = conv_w.shape
    T = min(PREP_T, S)
    rb = T // HALO
    return pl.pallas_call(
        _prep_kernel,
        grid=(B, S // T),
        in_specs=[
            pl.BlockSpec((1, T, W), lambda b, t: (b, t, 2)),
            pl.BlockSpec((1, HALO, W), lambda b, t: (b, jnp.maximum(t * rb - 1, 0), 2)),
            pl.BlockSpec((K, W), lambda b, t: (0, 0)),
        ],
        out_specs=pl.BlockSpec((1, T, W), lambda b, t: (b, t, 0)),
        out_shape=jax.ShapeDtypeStruct((B, S, W), BF16),
        scratch_shapes=[pltpu.VMEM((HALO + T, W), F32)],
        compiler_params=_params("parallel", "parallel"),
        name="qkv_prep",
    )(proj, proj, conv_w)


def _delta_kernel(qkv_ref, z_ref, sm_ref, alog_ref, dt_ref, ng_ref, o_ref, s_scr):
    t = pl.program_id(1)
    T = qkv_ref.shape[1]
    dm = HEADS * HEAD_DIM
    C = DELTA_CHUNK
    nchunk = T // C
    heads = range(HEADS)
    bf = lambda a: a.astype(BF16)

    @pl.when(t == 0)
    def _():
        s_scr[...] = jnp.zeros_like(s_scr)

    sm = sm_ref[0]
    beta_all = _sigmoid(sm)
    g_all = -jnp.exp(alog_ref[...]) * _softplus(sm + dt_ref[...])

    ri = lax.broadcasted_iota(jnp.int32, (T, T), 0)
    ci = lax.broadcasted_iota(jnp.int32, (T, T), 1)
    same_block = lambda size: jnp.right_shift(ri, size.bit_length() - 1) == jnp.right_shift(ci, size.bit_length() - 1)
    chunk = same_block(C)
    incl = chunk & (ri >= ci)
    strict = chunk & (ri > ci)
    eye = jnp.where(ri == ci, 1.0, 0.0)
    tri = bf(jnp.where(incl, 1.0, 0.0))

    g_hi = bf(g_all)
    r1 = g_all - g_hi.astype(F32)
    g_mid = bf(r1)
    g_lo = bf(r1 - g_mid.astype(F32))
    gcum = _dot(tri, g_hi) + _dot(tri, g_mid) + _dot(tri, g_lo)
    glast = jnp.concatenate(
        [jnp.broadcast_to(gcum[c * C + C - 1:c * C + C, :], (C, SMALL)) for c in range(nchunk)], axis=0)
    eg_all = jnp.exp(gcum)
    kdec_all = jnp.exp(glast - gcum)
    egl_all = jnp.exp(glast)
    gcum_t = gcum.T

    col = lambda a, h: a[:, HEADS + h:HEADS + h + 1]
    kf = [qkv_ref[0, :, dm + h * HEAD_DIM:dm + (h + 1) * HEAD_DIM] for h in heads]
    qf = [qkv_ref[0, :, h * HEAD_DIM:(h + 1) * HEAD_DIM].astype(F32) for h in heads]
    kb = [kf[h].astype(F32) * beta_all[:, h:h + 1] for h in heads]
    sc = [lax.dot_general(bf(jnp.concatenate([kb[h], qf[h]], axis=0)), kf[h], (((1,), (1,)), ((), ())),
                          preferred_element_type=F32) for h in heads]
    dec = [jnp.exp(jnp.where(incl, col(gcum, h) - gcum_t[HEADS + h:HEADS + h + 1, :], -jnp.inf)) for h in heads]
    m = [jnp.where(strict, sc[h][:T] * dec[h], 0.0) for h in heads]
    attn = [jnp.where(incl, sc[h][T:] * dec[h], 0.0) for h in heads]
    rhs = [bf(jnp.concatenate(
        [qkv_ref[0, :, 2 * dm + h * HEAD_DIM:2 * dm + (h + 1) * HEAD_DIM].astype(F32) * beta_all[:, h:h + 1],
         kb[h] * col(eg_all, h)], axis=1)) for h in heads]

    base = 8
    inner = same_block(base)
    p = [bf(jnp.where(inner, m[h], 0.0)) for h in heads]
    inv = [eye - jnp.where(inner, m[h], 0.0) for h in heads]
    size = 2
    while size < base:
        p = [bf(_dot(p[h], p[h])) for h in heads]
        inv = [inv[h] + _dot(bf(inv[h]), p[h]) for h in heads]
        size *= 2
    size = base
    while size < C:
        outer = same_block(2 * size)
        off = [bf(jnp.where(outer & jnp.logical_not(inner), m[h], 0.0)) for h in heads]
        invb = [bf(inv[h]) for h in heads]
        half = [bf(_dot(invb[h], off[h])) for h in heads]
        inv = [inv[h] - _dot(half[h], invb[h]) for h in heads]
        inner = outer
        size *= 2
    x = [_dot(bf(inv[h]), rhs[h]) for h in heads]

    qg = [qf[h] * col(eg_all, h) for h in heads]
    kd_t = [(kf[h].astype(F32) * col(kdec_all, h)).T for h in heads]
    state = [s_scr[h] for h in heads]
    outs = [[] for _ in heads]
    for c in range(nchunk):
        rows = slice(c * C, (c + 1) * C)
        a = [bf(jnp.concatenate([x[h][rows, HEAD_DIM:], qg[h][rows]], axis=0)) for h in heads]
        r = [_dot(a[h], bf(state[h])) for h in heads]
        v_new = [x[h][rows, :HEAD_DIM] - r[h][:C] for h in heads]
        pieces = lambda vn: ([jnp.zeros((c * C, HEAD_DIM), F32)] if c > 0 else []) + [vn] + (
            [jnp.zeros((T - (c + 1) * C, HEAD_DIM), F32)] if c < nchunk - 1 else [])
        v_pad = [bf(jnp.concatenate(pieces(v_new[h]), axis=0)) for h in heads]
        lb = [bf(jnp.concatenate([attn[h][rows], kd_t[h]], axis=0)) for h in heads]
        r2 = [_dot(lb[h], v_pad[h]) for h in heads]
        for h in heads:
            outs[h].append(r[h][C:] + r2[h][:C])
        state = [state[h] * egl_all[c * C:c * C + 1, HEADS + h:HEADS + h + 1] + r2[h][C:] for h in heads]
    for h in heads:
        hs = slice(h * HEAD_DIM, (h + 1) * HEAD_DIM)
        s_scr[h] = state[h]
        o = jnp.concatenate(outs[h], axis=0)
        zf = z_ref[0, :, hs].astype(F32)
        o_ref[0, :, hs] = bf(_rms(o) * ng_ref[...] * _silu(zf))


def _delta_call(qkv, proj, small, alog_row, dt_row, gdn_g, z_blk):
    B, S, W = qkv.shape
    dm = W // 3
    T = min(DELTA_T, S)
    return pl.pallas_call(
        _delta_kernel,
        grid=(B, S // T),
        in_specs=[
            pl.BlockSpec((1, T, W), lambda b, t: (b, t, 0)),
            pl.BlockSpec((1, T, dm), lambda b, t: (b, t, z_blk)),
            pl.BlockSpec((1, T, SMALL), lambda b, t: (b, t, 0)),
            pl.BlockSpec((1, SMALL), lambda b, t: (0, 0)),
            pl.BlockSpec((1, SMALL), lambda b, t: (0, 0)),
            pl.BlockSpec((1, HEAD_DIM), lambda b, t: (0, 0)),
        ],
        out_specs=pl.BlockSpec((1, T, dm), lambda b, t: (b, t, 0)),
        out_shape=jax.ShapeDtypeStruct((B, S, dm), BF16),
        scratch_shapes=[pltpu.VMEM((HEADS, HEAD_DIM, HEAD_DIM), F32)],
        compiler_params=_params("parallel", "arbitrary"),
        name="delta",
    )(qkv, proj, small, alog_row, dt_row, gdn_g)


def _mix_kernel(h_ref, gates_ref, xa_ref, ba_ref, ca_ref, xah_ref, cah_ref, uc_ref, vc_ref, yb_ref,
                gt_ref, ng_ref, cw_ref, lng_ref, lnb_ref, wsp_ref, bsp_ref, wb_ref, wo_ref,
                o_ref, ext_scr, yc_scr):
    t = pl.program_id(1)
    T = h_ref.shape[1]
    D = h_ref.shape[2]
    SC = SPATIAL_CHUNK
    nchunk = T // SC

    p = ca_ref[0].astype(F32) * xa_ref[0].astype(F32)
    ph = jnp.where(t > 0, cah_ref[0].astype(F32) * xah_ref[0].astype(F32), 0.0)
    ext_scr[0:HALO, :] = ph
    ext_scr[HALO:HALO + T, :] = p
    K = cw_ref.shape[0]
    conv = cw_ref[K - 1:K, :] * p
    for s in range(1, K):
        conv = conv + cw_ref[K - 1 - s:K - s, :] * ext_scr[HALO - s:HALO - s + T, :]
    y_a = (ba_ref[0].astype(F32) * conv).astype(BF16)

    v = _gelu_tanh(vc_ref[0].astype(F32))
    vcen = v - jnp.mean(v, axis=-1, keepdims=True)
    vn = vcen * lax.rsqrt(jnp.mean(vcen * vcen, axis=-1, keepdims=True) + EPS)
    vn = (vn * lng_ref[...] + lnb_ref[...]).astype(BF16)
    ri = lax.broadcasted_iota(jnp.int32, (SC, SC), 0)
    ci = lax.broadcasted_iota(jnp.int32, (SC, SC), 1)
    causal = ri >= ci
    gw = SC
    for g in range(GROUPS):
        gs = slice(g * gw, (g + 1) * gw)
        wg = jnp.where(causal, wsp_ref[g], 0.0).astype(BF16)
        rhs = jnp.concatenate([vn[c * SC:(c + 1) * SC, gs] for c in range(nchunk)], axis=1)
        mx = _dot(wg, rhs)
        mixed = jnp.concatenate([mx[:, c * gw:(c + 1) * gw] for c in range(nchunk)], axis=0)
        bias = jnp.concatenate([bsp_ref[:, gs]] * nchunk, axis=0)
        u = _gelu_tanh(uc_ref[0, :, gs].astype(F32))
        yc_scr[:, gs] = (u * (mixed + bias)).astype(BF16)

    gates = gates_ref[0]
    merged = _sigmoid(gates[:, 0:D].astype(F32)) * _dot(y_a, wb_ref[0])
    merged = merged + _sigmoid(gates[:, D:2 * D].astype(F32)) * _dot(yb_ref[0], wb_ref[1])
    merged = merged + _sigmoid(gates[:, 2 * D:3 * D].astype(F32)) * _dot(yc_scr[...], wb_ref[2])
    y = _dot(merged.astype(BF16), wo_ref[...])
    o_ref[0] = h_ref[0] + gt_ref[0] * (_rms(y) * ng_ref[0])


def _mix_call(h, proj, yb, ada, ng, conv_a, ln_g, ln_b, w_sp, b_sp, wb, wo):
    B, S, D = h.shape
    dm = D // 2
    lay = _proj_layout(D)
    T = min(MIX_T, S)
    rb = T // HALO
    K = conv_a.shape[0]
    one = pl.Buffered(1)
    tile = lambda blk: pl.BlockSpec((1, T, dm), lambda b, t: (b, t, blk))
    halo = lambda blk: pl.BlockSpec((1, HALO, dm), lambda b, t: (b, jnp.maximum(t * rb - 1, 0), blk))
    const = lambda shape: pl.BlockSpec(shape, lambda b, t: (0,) * len(shape), pipeline_mode=one)
    return pl.pallas_call(
        _mix_kernel,
        grid=(B, S // T),
        in_specs=[
            pl.BlockSpec((1, T, D), lambda b, t: (b, t, 0)),
            pl.BlockSpec((1, T, 3 * D), lambda b, t: (b, t, 0)),
            tile(lay["xa"]), tile(lay["ba"]), tile(lay["ca"]),
            halo(lay["xa"]), halo(lay["ca"]),
            tile(lay["uc"]), tile(lay["vc"]),
            pl.BlockSpec((1, T, dm), lambda b, t: (b, t, 0)),
            pl.BlockSpec((1, 1, D), lambda b, t: (b * N_ADA + 5, 0, 0)),
            pl.BlockSpec((1, 1, D), lambda b, t: (3, 0, 0)),
            const((K, dm)), const((1, dm)), const((1, dm)),
            const((GROUPS, SPATIAL_CHUNK, SPATIAL_CHUNK)), const((SPATIAL_CHUNK, dm)),
            const((3, dm, D)), const((D, D)),
        ],
        out_specs=pl.BlockSpec((1, T, D), lambda b, t: (b, t, 0)),
        out_shape=jax.ShapeDtypeStruct((B, S, D), F32),
        scratch_shapes=[pltpu.VMEM((HALO + T, dm), F32), pltpu.VMEM((T, dm), BF16)],
        compiler_params=_params("parallel", "parallel"),
        name="mix",
    )(h, proj, proj, proj, proj, proj, proj, proj, proj, yb, ada, ng,
      conv_a, ln_g, ln_b, w_sp, b_sp, wb, wo)


def _pad_axis(w, axis, mult):
    pad = (-w.shape[axis]) % mult
    widths = [(0, pad) if a == axis else (0, 0) for a in range(w.ndim)]
    return jnp.pad(w, widths) if pad else w


def kernel(x, c, w_ada, b_ada, norm_g, ffn_w_gate, ffn_w_up, ffn_w_down, w_in, conv_a, conv_qkv, a_log,
           dt_bias, gdn_norm_g, ln_v_g, ln_v_b, w_spatial, b_spatial, w_branch, w_o):
    B, S, D = x.shape
    L = w_ada.shape[0]
    dm = D // 2
    assert dm == HEADS * HEAD_DIM and dm == GROUPS * SPATIAL_CHUNK
    lay = _proj_layout(D)

    ada_all = _ada_call(c, w_ada, b_ada)

    o_qkv, o_z, o_small = 3 * dm, 6 * dm, 7 * dm
    o_uc = o_small + 2 * HEADS
    o_g = o_uc + 2 * dm

    h = x
    for l in range(L):
        ada = ada_all[l].reshape(B * N_ADA, 1, D)
        ng = norm_g[l].reshape(-1, 1, D)

        def ffn_weights(s):
            wg = _pad_axis(ffn_w_gate[l, s].astype(BF16), 1, FFN_TF)
            wu = _pad_axis(ffn_w_up[l, s].astype(BF16), 1, FFN_TF)
            wd = _pad_axis(ffn_w_down[l, s].astype(BF16), 0, FFN_TF)
            return wg, wu, wd

        h = _ffn_call(h, ada, ng, 0, *ffn_weights(0))

        wi = w_in[l]
        w_main = jnp.concatenate(
            [wi[:, o_g:], wi[:, o_qkv:o_z], wi[:, :o_qkv], wi[:, o_z:o_small], wi[:, o_uc:o_g]], axis=1).astype(BF16)
        w_small = _pad_axis(wi[:, o_small:o_uc], 1, SMALL).astype(BF16)
        proj, small = _proj_call(h, ada, ng, w_main, w_small)

        qkv = _prep_call(proj, conv_qkv[l])
        lane_pad = lambda v: jnp.pad(v, (HEADS, SMALL - 2 * HEADS)).reshape(1, SMALL)
        yb = _delta_call(qkv, proj, small, lane_pad(a_log[l]), lane_pad(dt_bias[l]),
                         gdn_norm_g[l].reshape(1, HEAD_DIM), lay["z"])

        b_exp = jnp.repeat(b_spatial[l].T, dm // GROUPS, axis=1)
        h = _mix_call(h, proj, yb, ada, ng, conv_a[l], ln_v_g[l].reshape(1, dm), ln_v_b[l].reshape(1, dm),
                      w_spatial[l], b_exp, w_branch[l].astype(BF16), w_o[l].astype(BF16))

        h = _ffn_call(h, ada, ng, 6, *ffn_weights(1))
    return h
```

```python
import functools

import jax
import jax.numpy as jnp
from jax import lax
from jax.experimental import pallas as pl
from jax.experimental.pallas import tpu as pltpu

F32 = jnp.float32
BF16 = jnp.bfloat16

EPS = 1e-6
HEADS = 8
HEAD_DIM = 128
DELTA_CHUNK = 64
GROUPS = 8
SPATIAL_CHUNK = 128
N_ADA = 9
N_NORMS = 6
HALO = 16
SMALL = 128

VMEM_LIMIT = 56 * 1024 * 1024

FFN_TM, FFN_TF = 1024, 256
PROJ_TM, PROJ_TN = 1024, 1024
PREP_T = 256
DELTA_T = 256
MIX_T = 256
ADA_TN = 1024
ROW_CHUNK = 16


def _sigmoid(x):
    return 1.0 / (1.0 + jnp.exp(-x))


def _silu(x):
    return x * _sigmoid(x)


def _gelu_tanh(x):
    return 0.5 * x * (1.0 + jnp.tanh(0.7978845608028654 * (x + 0.044715 * (x * x * x))))


def _softplus(x):
    return jnp.maximum(x, 0.0) + jnp.log(1.0 + jnp.exp(-jnp.abs(x)))


def _rms(x):
    return x * lax.rsqrt(jnp.mean(x * x, axis=-1, keepdims=True) + EPS)


def _dot(a, b):
    return jnp.dot(a, b, preferred_element_type=F32)


def _for_row_chunks(n_rows, fn, unroll):
    def body(r, carry):
        fn(pl.ds(pl.multiple_of(r * ROW_CHUNK, ROW_CHUNK), ROW_CHUNK))
        return carry
    lax.fori_loop(0, n_rows // ROW_CHUNK, body, 0, unroll=unroll)


def _row_rsqrt_pass(load, rs_scr, n_rows, width):
    lanes = rs_scr.shape[1]

    def partial_sums(rows):
        sq = load(rows)
        sq = sq * sq
        acc = sq[:, 0:lanes]
        for c in range(1, width // lanes):
            acc = acc + sq[:, c * lanes:(c + 1) * lanes]
        rs_scr[rows, :] = acc

    _for_row_chunks(n_rows, partial_sums, unroll=4)
    ms = jnp.sum(rs_scr[...], axis=-1, keepdims=True) / width
    rs_scr[...] = jnp.broadcast_to(lax.rsqrt(ms + EPS), rs_scr.shape)


def _lanes(rs, width):
    return jnp.tile(rs, (1, width // rs.shape[1]))


def _rows(v, width):
    return jnp.broadcast_to(v, (ROW_CHUNK, width))


def _params(*sem):
    return pltpu.CompilerParams(dimension_semantics=sem, vmem_limit_bytes=VMEM_LIMIT)


def _ada_kernel(c_ref, w_ref, b_ref, o_ref):
    c = c_ref[...]
    act = _silu(c).astype(BF16)
    o_ref[0] = _dot(act, w_ref[0].astype(BF16)) + b_ref[0]


def _ada_call(c, w_ada, b_ada):
    L, D, N = w_ada.shape
    B = c.shape[0]
    return pl.pallas_call(
        _ada_kernel,
        grid=(L, N // ADA_TN),
        in_specs=[
            pl.BlockSpec((B, D), lambda l, j: (0, 0)),
            pl.BlockSpec((1, D, ADA_TN), lambda l, j: (l, 0, j)),
            pl.BlockSpec((1, 1, ADA_TN), lambda l, j: (l, 0, j)),
        ],
        out_specs=pl.BlockSpec((1, B, ADA_TN), lambda l, j: (l, 0, j)),
        out_shape=jax.ShapeDtypeStruct((L, B, N), F32),
        compiler_params=_params("parallel", "parallel"),
        name="ada",
    )(c, w_ada, b_ada.reshape(L, 1, N))


def _ffn_kernel(h_ref, sh_ref, sc_ref, gt_ref, gpre_ref, gpost_ref, wg_ref, wu_ref, wd_ref,
                o_ref, n_scr, rs_scr, aff_scr):
    j = pl.program_id(2)
    tm, D = h_ref.shape[1], h_ref.shape[2]

    @pl.when(j == 0)
    def _():
        aff_scr[0] = _rows(gpre_ref[0] * (1.0 + sc_ref[0]), D)
        aff_scr[1] = _rows(sh_ref[0], D)
        _row_rsqrt_pass(lambda rows: h_ref[0, rows, :], rs_scr, tm, D)

        def prologue(rows):
            n = h_ref[0, rows, :] * _lanes(rs_scr[rows, :], D) * aff_scr[0] + aff_scr[1]
            n_scr[rows, :] = n.astype(BF16)
            o_ref[0, rows, :] = jnp.zeros((ROW_CHUNK, D), F32)

        _for_row_chunks(tm, prologue, unroll=2)

    n = n_scr[...]
    gate = _dot(n, wg_ref[...])
    up = _dot(n, wu_ref[...])
    hid = (_silu(gate) * up).astype(BF16)
    o_ref[0] += _dot(hid, wd_ref[...])

    @pl.when(j == pl.num_programs(2) - 1)
    def _():
        aff_scr[0] = _rows((0.5 * gt_ref[0]) * gpost_ref[0], D)
        _row_rsqrt_pass(lambda rows: o_ref[0, rows, :], rs_scr, tm, D)

        def epilogue(rows):
            r = o_ref[0, rows, :] * _lanes(rs_scr[rows, :], D)
            o_ref[0, rows, :] = h_ref[0, rows, :] + r * aff_scr[0]

        _for_row_chunks(tm, epilogue, unroll=2)


def _ffn_call(h, ada, ng, l, s, wg, wu, wd):
    B, S, D = h.shape
    Fp = wg.shape[-1]
    tm, tf = min(FFN_TM, S), FFN_TF
    k0 = 6 * s
    ada_spec = lambda k: pl.BlockSpec((1, 1, D), lambda b, i, j: ((l * B + b) * N_ADA + k, 0, 0))
    ng_spec = lambda k: pl.BlockSpec((1, 1, D), lambda b, i, j: (l * N_NORMS + k, 0, 0))
    npre = 4 * s
    return pl.pallas_call(
        _ffn_kernel,
        grid=(B, S // tm, Fp // tf),
        in_specs=[
            pl.BlockSpec((1, tm, D), lambda b, i, j: (b, i, 0)),
            ada_spec(k0), ada_spec(k0 + 1), ada_spec(k0 + 2),
            ng_spec(npre), ng_spec(npre + 1),
            pl.BlockSpec((None, None, D, tf), lambda b, i, j: (l, s, 0, j)),
            pl.BlockSpec((None, None, D, tf), lambda b, i, j: (l, s, 0, j)),
            pl.BlockSpec((None, None, tf, D), lambda b, i, j: (l, s, j, 0)),
        ],
        out_specs=pl.BlockSpec((1, tm, D), lambda b, i, j: (b, i, 0)),
        out_shape=jax.ShapeDtypeStruct((B, S, D), F32),
        scratch_shapes=[pltpu.VMEM((tm, D), BF16), pltpu.VMEM((tm, 128), F32),
                        pltpu.VMEM((2, ROW_CHUNK, D), F32)],
        compiler_params=_params("parallel", "parallel", "arbitrary"),
        name="ffn",
    )(h, ada, ada, ada, ng, ng, wg, wu, wd)


def _proj_kernel(h_ref, sh_ref, sc_ref, g_ref, w_ref, ws_ref, o_ref, os_ref, n_scr):
    j = pl.program_id(2)

    @pl.when(j == 0)
    def _():
        n = (_rms(h_ref[0]) * g_ref[0] * (1.0 + sc_ref[0]) + sh_ref[0]).astype(BF16)
        n_scr[...] = n
        os_ref[0] = _dot(n, ws_ref[...])

    o_ref[0] = _dot(n_scr[...], w_ref[...]).astype(BF16)


def _proj_call(h, ada, ng, l, w_main, w_small):
    B, S, D = h.shape
    N = w_main.shape[-1]
    tm, tn = min(PROJ_TM, S), PROJ_TN
    ada_spec = lambda k: pl.BlockSpec((1, 1, D), lambda b, i, j: ((l * B + b) * N_ADA + k, 0, 0))
    return pl.pallas_call(
        _proj_kernel,
        grid=(B, S // tm, N // tn),
        in_specs=[
            pl.BlockSpec((1, tm, D), lambda b, i, j: (b, i, 0)),
            ada_spec(3), ada_spec(4),
            pl.BlockSpec((1, 1, D), lambda b, i, j: (l * N_NORMS + 2, 0, 0)),
            pl.BlockSpec((None, D, tn), lambda b, i, j: (l, 0, j)),
            pl.BlockSpec((None, D, SMALL), lambda b, i, j: (l, 0, 0)),
        ],
        out_specs=[
            pl.BlockSpec((1, tm, tn), lambda b, i, j: (b, i, j)),
            pl.BlockSpec((1, tm, SMALL), lambda b, i, j: (b, i, 0)),
        ],
        out_shape=[jax.ShapeDtypeStruct((B, S, N), BF16), jax.ShapeDtypeStruct((B, S, SMALL), F32)],
        scratch_shapes=[pltpu.VMEM((tm, D), BF16)],
        compiler_params=_params("parallel", "parallel", "arbitrary"),
        name="proj",
    )(h, ada, ada, ng, w_main, w_small)


def _proj_layout(D):
    dm = D // 2
    gates = 0
    qkv = 3 * D
    rest = qkv + 3 * dm
    blk = rest // dm
    return dict(xa=blk, ba=blk + 1, ca=blk + 2, z=blk + 3, uc=blk + 4, vc=blk + 5)


def _prep_kernel(x_ref, halo_ref, w_ref, o_ref, ext_scr):
    t = pl.program_id(1)
    T = x_ref.shape[1]
    dm = x_ref.shape[2] // 3
    x = x_ref[0].astype(F32)
    halo = jnp.where(t > 0, halo_ref[0].astype(F32), 0.0)
    ext_scr[0:HALO, :] = halo
    ext_scr[HALO:HALO + T, :] = x
    K = w_ref.shape[0]
    acc = w_ref[K - 1:K, :] * x
    for s in range(1, K):
        acc = acc + w_ref[K - 1 - s:K - s, :] * ext_scr[HALO - s:HALO - s + T, :]
    y = _silu(acc)
    for i in range(2 * HEADS):
        sl = slice(i * HEAD_DIM, (i + 1) * HEAD_DIM)
        v = y[:, sl]
        v = v * lax.rsqrt(jnp.sum(v * v, axis=-1, keepdims=True) + EPS)
        if i < HEADS:
            v = v * (HEAD_DIM ** -0.5)
        o_ref[0, :, sl] = v.astype(BF16)
    o_ref[0, :, 2 * dm:3 * dm] = y[:, 2 * dm:3 * dm].astype(BF16)


def _prep_call(proj, l, conv_w):
    B, S, _ = proj.shape
    _, K, W = conv_w.shape
    T = min(PREP_T, S)
    rb = T // HALO
    return pl.pallas_call(
        _prep_kernel,
        grid=(B, S // T),
        in_specs=[
            pl.BlockSpec((1, T, W), lambda b, t: (b, t, 2)),
            pl.BlockSpec((1, HALO, W), lambda b, t: (b, jnp.maximum(t * rb - 1, 0), 2)),
            pl.BlockSpec((None, K, W), lambda b, t: (l, 0, 0)),
        ],
        out_specs=pl.BlockSpec((1, T, W), lambda b, t: (b, t, 0)),
        out_shape=jax.ShapeDtypeStruct((B, S, W), BF16),
        scratch_shapes=[pltpu.VMEM((HALO + T, W), F32)],
        compiler_params=_params("parallel", "parallel"),
        name="qkv_prep",
    )(proj, proj, conv_w)


def _delta_kernel(qkv_ref, z_ref, sm_ref, alog_ref, dt_ref, ng_ref, o_ref, s_scr):
    t = pl.program_id(1)
    T = qkv_ref.shape[1]
    dm = HEADS * HEAD_DIM
    C = DELTA_CHUNK
    nchunk = T // C
    heads = range(HEADS)
    bf = lambda a: a.astype(BF16)

    @pl.when(t == 0)
    def _():
        s_scr[...] = jnp.zeros_like(s_scr)

    sm = sm_ref[0]
    beta_all = _sigmoid(sm)
    g_all = -jnp.exp(alog_ref[...]) * _softplus(sm + dt_ref[...])

    ri = lax.broadcasted_iota(jnp.int32, (T, T), 0)
    ci = lax.broadcasted_iota(jnp.int32, (T, T), 1)
    same_block = lambda size: jnp.right_shift(ri, size.bit_length() - 1) == jnp.right_shift(ci, size.bit_length() - 1)
    chunk = same_block(C)
    incl = chunk & (ri >= ci)
    strict = chunk & (ri > ci)
    eye = jnp.where(ri == ci, 1.0, 0.0)
    tri = bf(jnp.where(incl, 1.0, 0.0))

    g_hi = bf(g_all)
    r1 = g_all - g_hi.astype(F32)
    g_mid = bf(r1)
    g_lo = bf(r1 - g_mid.astype(F32))
    gcum = _dot(tri, g_hi) + _dot(tri, g_mid) + _dot(tri, g_lo)
    glast = jnp.concatenate(
        [jnp.broadcast_to(gcum[c * C + C - 1:c * C + C, :], (C, SMALL)) for c in range(nchunk)], axis=0)
    eg_all = jnp.exp(gcum)
    kdec_all = jnp.exp(glast - gcum)
    egl_all = jnp.exp(glast)
    gcum_t = gcum.T

    col = lambda a, h: a[:, HEADS + h:HEADS + h + 1]
    kf = [qkv_ref[0, :, dm + h * HEAD_DIM:dm + (h + 1) * HEAD_DIM] for h in heads]
    qf = [qkv_ref[0, :, h * HEAD_DIM:(h + 1) * HEAD_DIM].astype(F32) for h in heads]
    kb = [kf[h].astype(F32) * beta_all[:, h:h + 1] for h in heads]
    sc = [lax.dot_general(bf(jnp.concatenate([kb[h], qf[h]], axis=0)), kf[h], (((1,), (1,)), ((), ())),
                          preferred_element_type=F32) for h in heads]
    dec = [jnp.exp(jnp.where(incl, col(gcum, h) - gcum_t[HEADS + h:HEADS + h + 1, :], -jnp.inf)) for h in heads]
    m = [jnp.where(strict, sc[h][:T] * dec[h], 0.0) for h in heads]
    attn = [jnp.where(incl, sc[h][T:] * dec[h], 0.0) for h in heads]
    rhs = [bf(jnp.concatenate(
        [qkv_ref[0, :, 2 * dm + h * HEAD_DIM:2 * dm + (h + 1) * HEAD_DIM].astype(F32) * beta_all[:, h:h + 1],
         kb[h] * col(eg_all, h)], axis=1)) for h in heads]

    base = 8
    inner = same_block(base)
    p = [bf(jnp.where(inner, m[h], 0.0)) for h in heads]
    inv = [eye - jnp.where(inner, m[h], 0.0) for h in heads]
    size = 2
    while size < base:
        p = [bf(_dot(p[h], p[h])) for h in heads]
        inv = [inv[h] + _dot(bf(inv[h]), p[h]) for h in heads]
        size *= 2
    size = base
    while size < C:
        outer = same_block(2 * size)
        off = [bf(jnp.where(outer & jnp.logical_not(inner), m[h], 0.0)) for h in heads]
        invb = [bf(inv[h]) for h in heads]
        half = [bf(_dot(invb[h], off[h])) for h in heads]
        inv = [inv[h] - _dot(half[h], invb[h]) for h in heads]
        inner = outer
        size *= 2
    x = [_dot(bf(inv[h]), rhs[h]) for h in heads]

    qg = [qf[h] * col(eg_all, h) for h in heads]
    kd_t = [(kf[h].astype(F32) * col(kdec_all, h)).T for h in heads]
    state = [s_scr[h] for h in heads]
    outs = [[] for _ in heads]
    for c in range(nchunk):
        rows = slice(c * C, (c + 1) * C)
        a = [bf(jnp.concatenate([x[h][rows, HEAD_DIM:], qg[h][rows]], axis=0)) for h in heads]
        r = [_dot(a[h], bf(state[h])) for h in heads]
        v_new = [x[h][rows, :HEAD_DIM] - r[h][:C] for h in heads]
        pieces = lambda vn: ([jnp.zeros((c * C, HEAD_DIM), F32)] if c > 0 else []) + [vn] + (
            [jnp.zeros((T - (c + 1) * C, HEAD_DIM), F32)] if c < nchunk - 1 else [])
        v_pad = [bf(jnp.concatenate(pieces(v_new[h]), axis=0)) for h in heads]
        lb = [bf(jnp.concatenate([attn[h][rows], kd_t[h]], axis=0)) for h in heads]
        r2 = [_dot(lb[h], v_pad[h]) for h in heads]
        for h in heads:
            outs[h].append(r[h][C:] + r2[h][:C])
        state = [state[h] * egl_all[c * C:c * C + 1, HEADS + h:HEADS + h + 1] + r2[h][C:] for h in heads]
    for h in heads:
        hs = slice(h * HEAD_DIM, (h + 1) * HEAD_DIM)
        s_scr[h] = state[h]
        o = jnp.concatenate(outs[h], axis=0)
        zf = z_ref[0, :, hs].astype(F32)
        o_ref[0, :, hs] = bf(_rms(o) * ng_ref[...] * _silu(zf))


def _delta_call(qkv, proj, small, l, alog_row, dt_row, gdn_g, z_blk):
    B, S, W = qkv.shape
    dm = W // 3
    T = min(DELTA_T, S)
    return pl.pallas_call(
        _delta_kernel,
        grid=(B, S // T),
        in_specs=[
            pl.BlockSpec((1, T, W), lambda b, t: (b, t, 0)),
            pl.BlockSpec((1, T, dm), lambda b, t: (b, t, z_blk)),
            pl.BlockSpec((1, T, SMALL), lambda b, t: (b, t, 0)),
            pl.BlockSpec((None, 1, SMALL), lambda b, t: (l, 0, 0)),
            pl.BlockSpec((None, 1, SMALL), lambda b, t: (l, 0, 0)),
            pl.BlockSpec((None, 1, HEAD_DIM), lambda b, t: (l, 0, 0)),
        ],
        out_specs=pl.BlockSpec((1, T, dm), lambda b, t: (b, t, 0)),
        out_shape=jax.ShapeDtypeStruct((B, S, dm), BF16),
        scratch_shapes=[pltpu.VMEM((HEADS, HEAD_DIM, HEAD_DIM), F32)],
        compiler_params=_params("parallel", "arbitrary"),
        name="delta",
    )(qkv, proj, small, alog_row, dt_row, gdn_g)


def _mix_kernel(h_ref, gates_ref, xa_ref, ba_ref, ca_ref, xah_ref, cah_ref, uc_ref, vc_ref, yb_ref,
                gt_ref, ng_ref, cw_ref, lng_ref, lnb_ref, wsp_ref, bsp_ref, wb_ref, wo_ref,
                o_ref, ext_scr, yc_scr):
    t = pl.program_id(1)
    T = h_ref.shape[1]
    D = h_ref.shape[2]
    SC = SPATIAL_CHUNK
    nchunk = T // SC

    p = ca_ref[0].astype(F32) * xa_ref[0].astype(F32)
    ph = jnp.where(t > 0, cah_ref[0].astype(F32) * xah_ref[0].astype(F32), 0.0)
    ext_scr[0:HALO, :] = ph
    ext_scr[HALO:HALO + T, :] = p
    K = cw_ref.shape[0]
    conv = cw_ref[K - 1:K, :] * p
    for s in range(1, K):
        conv = conv + cw_ref[K - 1 - s:K - s, :] * ext_scr[HALO - s:HALO - s + T, :]
    y_a = (ba_ref[0].astype(F32) * conv).astype(BF16)

    v = _gelu_tanh(vc_ref[0].astype(F32))
    vcen = v - jnp.mean(v, axis=-1, keepdims=True)
    vn = vcen * lax.rsqrt(jnp.mean(vcen * vcen, axis=-1, keepdims=True) + EPS)
    vn = (vn * lng_ref[...] + lnb_ref[...]).astype(BF16)
    ri = lax.broadcasted_iota(jnp.int32, (SC, SC), 0)
    ci = lax.broadcasted_iota(jnp.int32, (SC, SC), 1)
    causal = ri >= ci
    gw = SC
    for g in range(GROUPS):
        gs = slice(g * gw, (g + 1) * gw)
        wg = jnp.where(causal, wsp_ref[g], 0.0).astype(BF16)
        rhs = jnp.concatenate([vn[c * SC:(c + 1) * SC, gs] for c in range(nchunk)], axis=1)
        mx = _dot(wg, rhs)
        mixed = jnp.concatenate([mx[:, c * gw:(c + 1) * gw] for c in range(nchunk)], axis=0)
        bias = jnp.concatenate([bsp_ref[:, gs]] * nchunk, axis=0)
        u = _gelu_tanh(uc_ref[0, :, gs].astype(F32))
        yc_scr[:, gs] = (u * (mixed + bias)).astype(BF16)

    gates = gates_ref[0]
    merged = _sigmoid(gates[:, 0:D].astype(F32)) * _dot(y_a, wb_ref[0])
    merged = merged + _sigmoid(gates[:, D:2 * D].astype(F32)) * _dot(yb_ref[0], wb_ref[1])
    merged = merged + _sigmoid(gates[:, 2 * D:3 * D].astype(F32)) * _dot(yc_scr[...], wb_ref[2])
    y = _dot(merged.astype(BF16), wo_ref[...])
    o_ref[0] = h_ref[0] + gt_ref[0] * (_rms(y) * ng_ref[0])


def _mix_call(h, proj, yb, ada, ng, l, conv_a, ln_g, ln_b, w_sp, b_sp, wb, wo):
    B, S, D = h.shape
    dm = D // 2
    lay = _proj_layout(D)
    T = min(MIX_T, S)
    rb = T // HALO
    K = conv_a.shape[1]
    one = pl.Buffered(1)
    tile = lambda blk: pl.BlockSpec((1, T, dm), lambda b, t: (b, t, blk))
    halo = lambda blk: pl.BlockSpec((1, HALO, dm), lambda b, t: (b, jnp.maximum(t * rb - 1, 0), blk))
    const = lambda shape: pl.BlockSpec((None,) + shape, lambda b, t: (l,) + (0,) * len(shape), pipeline_mode=one)
    return pl.pallas_call(
        _mix_kernel,
        grid=(B, S // T),
        in_specs=[
            pl.BlockSpec((1, T, D), lambda b, t: (b, t, 0)),
            pl.BlockSpec((1, T, 3 * D), lambda b, t: (b, t, 0)),
            tile(lay["xa"]), tile(lay["ba"]), tile(lay["ca"]),
            halo(lay["xa"]), halo(lay["ca"]),
            tile(lay["uc"]), tile(lay["vc"]),
            pl.BlockSpec((1, T, dm), lambda b, t: (b, t, 0)),
            pl.BlockSpec((1, 1, D), lambda b, t: ((l * B + b) * N_ADA + 5, 0, 0)),
            pl.BlockSpec((1, 1, D), lambda b, t: (l * N_NORMS + 3, 0, 0)),
            const((K, dm)), const((1, dm)), const((1, dm)),
            const((GROUPS, SPATIAL_CHUNK, SPATIAL_CHUNK)), const((SPATIAL_CHUNK, dm)),
            const((3, dm, D)), const((D, D)),
        ],
        out_specs=pl.BlockSpec((1, T, D), lambda b, t: (b, t, 0)),
        out_shape=jax.ShapeDtypeStruct((B, S, D), F32),
        scratch_shapes=[pltpu.VMEM((HALO + T, dm), F32), pltpu.VMEM((T, dm), BF16)],
        compiler_params=_params("parallel", "parallel"),
        name="mix",
    )(h, proj, proj, proj, proj, proj, proj, proj, proj, yb, ada, ng,
      conv_a, ln_g, ln_b, w_sp, b_sp, wb, wo)


def _pad_axis(w, axis, mult):
    pad = (-w.shape[axis]) % mult
    widths = [(0, pad) if a == axis else (0, 0) for a in range(w.ndim)]
    return jnp.pad(w, widths) if pad else w


def kernel(x, c, w_ada, b_ada, norm_g, ffn_w_gate, ffn_w_up, ffn_w_down, w_in, conv_a, conv_qkv, a_log,
           dt_bias, gdn_norm_g, ln_v_g, ln_v_b, w_spatial, b_spatial, w_branch, w_o):
    B, S, D = x.shape
    L = w_ada.shape[0]
    dm = D // 2
    assert dm == HEADS * HEAD_DIM and dm == GROUPS * SPATIAL_CHUNK
    lay = _proj_layout(D)

    ada = _ada_call(c, w_ada, b_ada).reshape(L * B * N_ADA, 1, D)
    ng = norm_g.reshape(L * N_NORMS, 1, D)

    wg = _pad_axis(ffn_w_gate.astype(BF16), 3, FFN_TF)
    wu = _pad_axis(ffn_w_up.astype(BF16), 3, FFN_TF)
    wd = _pad_axis(ffn_w_down.astype(BF16), 2, FFN_TF)
    o_qkv, o_z, o_small = 3 * dm, 6 * dm, 7 * dm
    o_uc = o_small + 2 * HEADS
    o_g = o_uc + 2 * dm
    w_main = jnp.concatenate([w_in[..., o_g:], w_in[..., o_qkv:o_z], w_in[..., :o_qkv], w_in[..., o_z:o_small],
                              w_in[..., o_uc:o_g]], axis=-1).astype(BF16)
    w_small = _pad_axis(w_in[..., o_small:o_uc], 2, SMALL).astype(BF16)
    wb = w_branch.astype(BF16)
    wo = w_o.astype(BF16)
    lane_pad = lambda v: jnp.pad(v, ((0, 0), (HEADS, SMALL - 2 * HEADS))).reshape(L, 1, SMALL)
    alog_rows, dt_rows = lane_pad(a_log), lane_pad(dt_bias)
    gdn_rows = gdn_norm_g.reshape(L, 1, HEAD_DIM)
    ln_g, ln_b = ln_v_g.reshape(L, 1, dm), ln_v_b.reshape(L, 1, dm)
    b_exp = jnp.repeat(jnp.swapaxes(b_spatial, 1, 2), dm // GROUPS, axis=2)

    h = x
    for l in range(L):
        h = _ffn_call(h, ada, ng, l, 0, wg, wu, wd)
        proj, small = _proj_call(h, ada, ng, l, w_main, w_small)
        qkv = _prep_call(proj, l, conv_qkv)
        yb = _delta_call(qkv, proj, small, l, alog_rows, dt_rows, gdn_rows, lay["z"])
        h = _mix_call(h, proj, yb, ada, ng, l, conv_a, ln_g, ln_b, w_spatial, b_exp, wb, wo)
        h = _ffn_call(h, ada, ng, l, 1, wg, wu, wd)
    return h
```

```python
import functools

import jax
import jax.numpy as jnp
from jax import lax
from jax.experimental import pallas as pl
from jax.experimental.pallas import tpu as pltpu

F32 = jnp.float32
BF16 = jnp.bfloat16

EPS = 1e-6
HEADS = 8
HEAD_DIM = 128
DELTA_CHUNK = 64
GROUPS = 8
SPATIAL_CHUNK = 128
N_ADA = 9
N_NORMS = 6
HALO = 16
SMALL = 128

VMEM_LIMIT = 62 * 1024 * 1024

FFN_TM, FFN_TF = 1024, 512
PROJ_TM, PROJ_TN = 1024, 1024
PREP_T = 256
DELTA_T = 256
MIX_T = 256
ADA_TN = 1024
ROW_CHUNK = 16


def _sigmoid(x):
    return 1.0 / (1.0 + jnp.exp(-x))


def _silu(x):
    return x * _sigmoid(x)


def _gelu_tanh(x):
    return 0.5 * x * (1.0 + jnp.tanh(0.7978845608028654 * (x + 0.044715 * (x * x * x))))


def _softplus(x):
    return jnp.maximum(x, 0.0) + jnp.log(1.0 + jnp.exp(-jnp.abs(x)))


def _rms(x):
    return x * lax.rsqrt(jnp.mean(x * x, axis=-1, keepdims=True) + EPS)


def _dot(a, b):
    return jnp.dot(a, b, preferred_element_type=F32)


def _for_row_chunks(n_rows, fn, unroll):
    def body(r, carry):
        fn(pl.ds(pl.multiple_of(r * ROW_CHUNK, ROW_CHUNK), ROW_CHUNK))
        return carry
    lax.fori_loop(0, n_rows // ROW_CHUNK, body, 0, unroll=unroll)


def _row_rsqrt_pass(load, rs_scr, n_rows, width):
    lanes = rs_scr.shape[1]

    def partial_sums(rows):
        sq = load(rows)
        sq = sq * sq
        acc = sq[:, 0:lanes]
        for c in range(1, width // lanes):
            acc = acc + sq[:, c * lanes:(c + 1) * lanes]
        rs_scr[rows, :] = acc

    _for_row_chunks(n_rows, partial_sums, unroll=4)
    ms = jnp.sum(rs_scr[...], axis=-1, keepdims=True) / width
    rs_scr[...] = jnp.broadcast_to(lax.rsqrt(ms + EPS), rs_scr.shape)


def _lanes(rs, width):
    return jnp.tile(rs, (1, width // rs.shape[1]))


def _rows(v, width):
    return jnp.broadcast_to(v, (ROW_CHUNK, width))


def _params(*sem):
    return pltpu.CompilerParams(dimension_semantics=sem, vmem_limit_bytes=VMEM_LIMIT)


def _ada_kernel(c_ref, w_ref, b_ref, o_ref):
    c = c_ref[...]
    act = _silu(c).astype(BF16)
    o_ref[0] = _dot(act, w_ref[0].astype(BF16)) + b_ref[0]


def _ada_call(c, w_ada, b_ada):
    L, D, N = w_ada.shape
    B = c.shape[0]
    return pl.pallas_call(
        _ada_kernel,
        grid=(L, N // ADA_TN),
        in_specs=[
            pl.BlockSpec((B, D), lambda l, j: (0, 0)),
            pl.BlockSpec((1, D, ADA_TN), lambda l, j: (l, 0, j)),
            pl.BlockSpec((1, 1, ADA_TN), lambda l, j: (l, 0, j)),
        ],
        out_specs=pl.BlockSpec((1, B, ADA_TN), lambda l, j: (l, 0, j)),
        out_shape=jax.ShapeDtypeStruct((L, B, N), F32),
        compiler_params=_params("parallel", "parallel"),
        name="ada",
    )(c, w_ada, b_ada.reshape(L, 1, N))


def _ffn_kernel(h_ref, sh_ref, sc_ref, gt_ref, gpre_ref, gpost_ref, wg_ref, wu_ref, wd_ref,
                o_ref, n_scr, rs_scr, aff_scr):
    j = pl.program_id(2)
    tm, D = h_ref.shape[1], h_ref.shape[2]

    @pl.when(j == 0)
    def _():
        aff_scr[0] = _rows(gpre_ref[0] * (1.0 + sc_ref[0]), D)
        aff_scr[1] = _rows(sh_ref[0], D)
        _row_rsqrt_pass(lambda rows: h_ref[0, rows, :], rs_scr, tm, D)

        def prologue(rows):
            n = h_ref[0, rows, :] * _lanes(rs_scr[rows, :], D) * aff_scr[0] + aff_scr[1]
            n_scr[rows, :] = n.astype(BF16)
            o_ref[0, rows, :] = jnp.zeros((ROW_CHUNK, D), F32)

        _for_row_chunks(tm, prologue, unroll=2)

    n = n_scr[...]
    gate = _dot(n, wg_ref[...])
    up = _dot(n, wu_ref[...])
    hid = (_silu(gate) * up).astype(BF16)
    o_ref[0] += _dot(hid, wd_ref[...])

    @pl.when(j == pl.num_programs(2) - 1)
    def _():
        aff_scr[0] = _rows((0.5 * gt_ref[0]) * gpost_ref[0], D)
        _row_rsqrt_pass(lambda rows: o_ref[0, rows, :], rs_scr, tm, D)

        def epilogue(rows):
            r = o_ref[0, rows, :] * _lanes(rs_scr[rows, :], D)
            o_ref[0, rows, :] = h_ref[0, rows, :] + r * aff_scr[0]

        _for_row_chunks(tm, epilogue, unroll=2)


def _ffn_call(h, ada, ng, l, s, wg, wu, wd):
    B, S, D = h.shape
    Fp = wg.shape[-1]
    tm, tf = min(FFN_TM, S), FFN_TF
    k0 = 6 * s
    ada_spec = lambda k: pl.BlockSpec((1, 1, D), lambda b, i, j: ((l * B + b) * N_ADA + k, 0, 0))
    ng_spec = lambda k: pl.BlockSpec((1, 1, D), lambda b, i, j: (l * N_NORMS + k, 0, 0))
    npre = 4 * s
    return pl.pallas_call(
        _ffn_kernel,
        grid=(B, S // tm, Fp // tf),
        in_specs=[
            pl.BlockSpec((1, tm, D), lambda b, i, j: (b, i, 0)),
            ada_spec(k0), ada_spec(k0 + 1), ada_spec(k0 + 2),
            ng_spec(npre), ng_spec(npre + 1),
            pl.BlockSpec((None, None, D, tf), lambda b, i, j: (l, s, 0, j)),
            pl.BlockSpec((None, None, D, tf), lambda b, i, j: (l, s, 0, j)),
            pl.BlockSpec((None, None, tf, D), lambda b, i, j: (l, s, j, 0)),
        ],
        out_specs=pl.BlockSpec((1, tm, D), lambda b, i, j: (b, i, 0)),
        out_shape=jax.ShapeDtypeStruct((B, S, D), F32),
        scratch_shapes=[pltpu.VMEM((tm, D), BF16), pltpu.VMEM((tm, 128), F32),
                        pltpu.VMEM((2, ROW_CHUNK, D), F32)],
        compiler_params=_params("parallel", "parallel", "arbitrary"),
        name="ffn",
    )(h, ada, ada, ada, ng, ng, wg, wu, wd)


def _proj_kernel(h_ref, sh_ref, sc_ref, g_ref, w_ref, ws_ref, o_ref, os_ref, n_scr):
    j = pl.program_id(2)

    @pl.when(j == 0)
    def _():
        n = (_rms(h_ref[0]) * g_ref[0] * (1.0 + sc_ref[0]) + sh_ref[0]).astype(BF16)
        n_scr[...] = n
        os_ref[0] = _dot(n, ws_ref[...])

    o_ref[0] = _dot(n_scr[...], w_ref[...]).astype(BF16)


def _proj_call(h, ada, ng, l, w_main, w_small):
    B, S, D = h.shape
    N = w_main.shape[-1]
    tm, tn = min(PROJ_TM, S), PROJ_TN
    ada_spec = lambda k: pl.BlockSpec((1, 1, D), lambda b, i, j: ((l * B + b) * N_ADA + k, 0, 0))
    return pl.pallas_call(
        _proj_kernel,
        grid=(B, S // tm, N // tn),
        in_specs=[
            pl.BlockSpec((1, tm, D), lambda b, i, j: (b, i, 0)),
            ada_spec(3), ada_spec(4),
            pl.BlockSpec((1, 1, D), lambda b, i, j: (l * N_NORMS + 2, 0, 0)),
            pl.BlockSpec((None, D, tn), lambda b, i, j: (l, 0, j)),
            pl.BlockSpec((None, D, SMALL), lambda b, i, j: (l, 0, 0)),
        ],
        out_specs=[
            pl.BlockSpec((1, tm, tn), lambda b, i, j: (b, i, j)),
            pl.BlockSpec((1, tm, SMALL), lambda b, i, j: (b, i, 0)),
        ],
        out_shape=[jax.ShapeDtypeStruct((B, S, N), BF16), jax.ShapeDtypeStruct((B, S, SMALL), F32)],
        scratch_shapes=[pltpu.VMEM((tm, D), BF16)],
        compiler_params=_params("parallel", "parallel", "arbitrary"),
        name="proj",
    )(h, ada, ada, ng, w_main, w_small)


def _proj_layout(D):
    dm = D // 2
    gates = 0
    qkv = 3 * D
    rest = qkv + 3 * dm
    blk = rest // dm
    return dict(xa=blk, ba=blk + 1, ca=blk + 2, z=blk + 3, uc=blk + 4, vc=blk + 5)


def _prep_kernel(x_ref, halo_ref, w_ref, o_ref, ext_scr):
    t = pl.program_id(1)
    T = x_ref.shape[1]
    dm = x_ref.shape[2] // 3
    x = x_ref[0].astype(F32)
    halo = jnp.where(t > 0, halo_ref[0].astype(F32), 0.0)
    ext_scr[0:HALO, :] = halo
    ext_scr[HALO:HALO + T, :] = x
    K = w_ref.shape[0]
    acc = w_ref[K - 1:K, :] * x
    for s in range(1, K):
        acc = acc + w_ref[K - 1 - s:K - s, :] * ext_scr[HALO - s:HALO - s + T, :]
    y = _silu(acc)
    for i in range(2 * HEADS):
        sl = slice(i * HEAD_DIM, (i + 1) * HEAD_DIM)
        v = y[:, sl]
        v = v * lax.rsqrt(jnp.sum(v * v, axis=-1, keepdims=True) + EPS)
        if i < HEADS:
            v = v * (HEAD_DIM ** -0.5)
        o_ref[0, :, sl] = v.astype(BF16)
    o_ref[0, :, 2 * dm:3 * dm] = y[:, 2 * dm:3 * dm].astype(BF16)


def _prep_call(proj, l, conv_w):
    B, S, _ = proj.shape
    _, K, W = conv_w.shape
    T = min(PREP_T, S)
    rb = T // HALO
    return pl.pallas_call(
        _prep_kernel,
        grid=(B, S // T),
        in_specs=[
            pl.BlockSpec((1, T, W), lambda b, t: (b, t, 2)),
            pl.BlockSpec((1, HALO, W), lambda b, t: (b, jnp.maximum(t * rb - 1, 0), 2)),
            pl.BlockSpec((None, K, W), lambda b, t: (l, 0, 0)),
        ],
        out_specs=pl.BlockSpec((1, T, W), lambda b, t: (b, t, 0)),
        out_shape=jax.ShapeDtypeStruct((B, S, W), BF16),
        scratch_shapes=[pltpu.VMEM((HALO + T, W), F32)],
        compiler_params=_params("parallel", "parallel"),
        name="qkv_prep",
    )(proj, proj, conv_w)


def _delta_kernel(qkv_ref, z_ref, sm_ref, alog_ref, dt_ref, ng_ref, o_ref, s_scr):
    t = pl.program_id(1)
    T = qkv_ref.shape[1]
    dm = HEADS * HEAD_DIM
    C = DELTA_CHUNK
    nchunk = T // C
    heads = range(HEADS)
    bf = lambda a: a.astype(BF16)

    @pl.when(t == 0)
    def _():
        s_scr[...] = jnp.zeros_like(s_scr)

    sm = sm_ref[0]
    beta_all = _sigmoid(sm)
    g_all = -jnp.exp(alog_ref[...]) * _softplus(sm + dt_ref[...])

    ri = lax.broadcasted_iota(jnp.int32, (T, T), 0)
    ci = lax.broadcasted_iota(jnp.int32, (T, T), 1)
    same_block = lambda size: jnp.right_shift(ri, size.bit_length() - 1) == jnp.right_shift(ci, size.bit_length() - 1)
    chunk = same_block(C)
    incl = chunk & (ri >= ci)
    strict = chunk & (ri > ci)
    eye = jnp.where(ri == ci, 1.0, 0.0)
    tri = bf(jnp.where(incl, 1.0, 0.0))

    g_hi = bf(g_all)
    r1 = g_all - g_hi.astype(F32)
    g_mid = bf(r1)
    g_lo = bf(r1 - g_mid.astype(F32))
    gcum = _dot(tri, g_hi) + _dot(tri, g_mid) + _dot(tri, g_lo)
    glast = jnp.concatenate(
        [jnp.broadcast_to(gcum[c * C + C - 1:c * C + C, :], (C, SMALL)) for c in range(nchunk)], axis=0)
    eg_all = jnp.exp(gcum)
    kdec_all = jnp.exp(glast - gcum)
    egl_all = jnp.exp(glast)
    gcum_t = gcum.T

    col = lambda a, h: a[:, HEADS + h:HEADS + h + 1]
    kf = [qkv_ref[0, :, dm + h * HEAD_DIM:dm + (h + 1) * HEAD_DIM] for h in heads]
    qf = [qkv_ref[0, :, h * HEAD_DIM:(h + 1) * HEAD_DIM].astype(F32) for h in heads]
    kb = [kf[h].astype(F32) * beta_all[:, h:h + 1] for h in heads]
    sc = [lax.dot_general(bf(jnp.concatenate([kb[h], qf[h]], axis=0)), kf[h], (((1,), (1,)), ((), ())),
                          preferred_element_type=F32) for h in heads]
    dec = [jnp.exp(jnp.where(incl, col(gcum, h) - gcum_t[HEADS + h:HEADS + h + 1, :], -jnp.inf)) for h in heads]
    m = [jnp.where(strict, sc[h][:T] * dec[h], 0.0) for h in heads]
    attn = [jnp.where(incl, sc[h][T:] * dec[h], 0.0) for h in heads]
    rhs = [bf(jnp.concatenate(
        [qkv_ref[0, :, 2 * dm + h * HEAD_DIM:2 * dm + (h + 1) * HEAD_DIM].astype(F32) * beta_all[:, h:h + 1],
         kb[h] * col(eg_all, h)], axis=1)) for h in heads]

    base = 8
    inner = same_block(base)
    p = [bf(jnp.where(inner, m[h], 0.0)) for h in heads]
    inv = [eye - jnp.where(inner, m[h], 0.0) for h in heads]
    size = 2
    while size < base:
        p = [bf(_dot(p[h], p[h])) for h in heads]
        inv = [inv[h] + _dot(bf(inv[h]), p[h]) for h in heads]
        size *= 2
    size = base
    while size < C:
        outer = same_block(2 * size)
        off = [bf(jnp.where(outer & jnp.logical_not(inner), m[h], 0.0)) for h in heads]
        invb = [bf(inv[h]) for h in heads]
        half = [bf(_dot(invb[h], off[h])) for h in heads]
        inv = [inv[h] - _dot(half[h], invb[h]) for h in heads]
        inner = outer
        size *= 2
    x = [_dot(bf(inv[h]), rhs[h]) for h in heads]

    qg = [qf[h] * col(eg_all, h) for h in heads]
    kd_t = [(kf[h].astype(F32) * col(kdec_all, h)).T for h in heads]
    state = [s_scr[h] for h in heads]
    outs = [[] for _ in heads]
    for c in range(nchunk):
        rows = slice(c * C, (c + 1) * C)
        a = [bf(jnp.concatenate([x[h][rows, HEAD_DIM:], qg[h][rows]], axis=0)) for h in heads]
        r = [_dot(a[h], bf(state[h])) for h in heads]
        v_new = [x[h][rows, :HEAD_DIM] - r[h][:C] for h in heads]
        pieces = lambda vn: ([jnp.zeros((c * C, HEAD_DIM), F32)] if c > 0 else []) + [vn] + (
            [jnp.zeros((T - (c + 1) * C, HEAD_DIM), F32)] if c < nchunk - 1 else [])
        v_pad = [bf(jnp.concatenate(pieces(v_new[h]), axis=0)) for h in heads]
        lb = [bf(jnp.concatenate([attn[h][rows], kd_t[h]], axis=0)) for h in heads]
        r2 = [_dot(lb[h], v_pad[h]) for h in heads]
        for h in heads:
            outs[h].append(r[h][C:] + r2[h][:C])
        state = [state[h] * egl_all[c * C:c * C + 1, HEADS + h:HEADS + h + 1] + r2[h][C:] for h in heads]
    for h in heads:
        hs = slice(h * HEAD_DIM, (h + 1) * HEAD_DIM)
        s_scr[h] = state[h]
        o = jnp.concatenate(outs[h], axis=0)
        zf = z_ref[0, :, hs].astype(F32)
        o_ref[0, :, hs] = bf(_rms(o) * ng_ref[...] * _silu(zf))


def _delta_call(qkv, proj, small, l, alog_row, dt_row, gdn_g, z_blk):
    B, S, W = qkv.shape
    dm = W // 3
    T = min(DELTA_T, S)
    return pl.pallas_call(
        _delta_kernel,
        grid=(B, S // T),
        in_specs=[
            pl.BlockSpec((1, T, W), lambda b, t: (b, t, 0)),
            pl.BlockSpec((1, T, dm), lambda b, t: (b, t, z_blk)),
            pl.BlockSpec((1, T, SMALL), lambda b, t: (b, t, 0)),
            pl.BlockSpec((None, 1, SMALL), lambda b, t: (l, 0, 0)),
            pl.BlockSpec((None, 1, SMALL), lambda b, t: (l, 0, 0)),
            pl.BlockSpec((None, 1, HEAD_DIM), lambda b, t: (l, 0, 0)),
        ],
        out_specs=pl.BlockSpec((1, T, dm), lambda b, t: (b, t, 0)),
        out_shape=jax.ShapeDtypeStruct((B, S, dm), BF16),
        scratch_shapes=[pltpu.VMEM((HEADS, HEAD_DIM, HEAD_DIM), F32)],
        compiler_params=_params("parallel", "arbitrary"),
        name="delta",
    )(qkv, proj, small, alog_row, dt_row, gdn_g)


def _mix_kernel(h_ref, gates_ref, xa_ref, ba_ref, ca_ref, xah_ref, cah_ref, uc_ref, vc_ref, yb_ref,
                gt_ref, ng_ref, cw_ref, lng_ref, lnb_ref, wsp_ref, bsp_ref, wb_ref, wo_ref,
                o_ref, ext_scr, yc_scr):
    t = pl.program_id(1)
    T = h_ref.shape[1]
    D = h_ref.shape[2]
    SC = SPATIAL_CHUNK
    nchunk = T // SC

    p = ca_ref[0].astype(F32) * xa_ref[0].astype(F32)
    ph = jnp.where(t > 0, cah_ref[0].astype(F32) * xah_ref[0].astype(F32), 0.0)
    ext_scr[0:HALO, :] = ph
    ext_scr[HALO:HALO + T, :] = p
    K = cw_ref.shape[0]
    conv = cw_ref[K - 1:K, :] * p
    for s in range(1, K):
        conv = conv + cw_ref[K - 1 - s:K - s, :] * ext_scr[HALO - s:HALO - s + T, :]
    y_a = (ba_ref[0].astype(F32) * conv).astype(BF16)

    v = _gelu_tanh(vc_ref[0].astype(F32))
    vcen = v - jnp.mean(v, axis=-1, keepdims=True)
    vn = vcen * lax.rsqrt(jnp.mean(vcen * vcen, axis=-1, keepdims=True) + EPS)
    vn = (vn * lng_ref[...] + lnb_ref[...]).astype(BF16)
    ri = lax.broadcasted_iota(jnp.int32, (SC, SC), 0)
    ci = lax.broadcasted_iota(jnp.int32, (SC, SC), 1)
    causal = ri >= ci
    gw = SC
    for g in range(GROUPS):
        gs = slice(g * gw, (g + 1) * gw)
        wg = jnp.where(causal, wsp_ref[g], 0.0).astype(BF16)
        rhs = jnp.concatenate([vn[c * SC:(c + 1) * SC, gs] for c in range(nchunk)], axis=1)
        mx = _dot(wg, rhs)
        mixed = jnp.concatenate([mx[:, c * gw:(c + 1) * gw] for c in range(nchunk)], axis=0)
        bias = jnp.concatenate([bsp_ref[:, gs]] * nchunk, axis=0)
        u = _gelu_tanh(uc_ref[0, :, gs].astype(F32))
        yc_scr[:, gs] = (u * (mixed + bias)).astype(BF16)

    gates = gates_ref[0]
    merged = _sigmoid(gates[:, 0:D].astype(F32)) * _dot(y_a, wb_ref[0])
    merged = merged + _sigmoid(gates[:, D:2 * D].astype(F32)) * _dot(yb_ref[0], wb_ref[1])
    merged = merged + _sigmoid(gates[:, 2 * D:3 * D].astype(F32)) * _dot(yc_scr[...], wb_ref[2])
    y = _dot(merged.astype(BF16), wo_ref[...])
    o_ref[0] = h_ref[0] + gt_ref[0] * (_rms(y) * ng_ref[0])


def _mix_call(h, proj, yb, ada, ng, l, conv_a, ln_g, ln_b, w_sp, b_sp, wb, wo):
    B, S, D = h.shape
    dm = D // 2
    lay = _proj_layout(D)
    T = min(MIX_T, S)
    rb = T // HALO
    K = conv_a.shape[1]
    one = pl.Buffered(1)
    tile = lambda blk: pl.BlockSpec((1, T, dm), lambda b, t: (b, t, blk))
    halo = lambda blk: pl.BlockSpec((1, HALO, dm), lambda b, t: (b, jnp.maximum(t * rb - 1, 0), blk))
    const = lambda shape: pl.BlockSpec((None,) + shape, lambda b, t: (l,) + (0,) * len(shape), pipeline_mode=one)
    return pl.pallas_call(
        _mix_kernel,
        grid=(B, S // T),
        in_specs=[
            pl.BlockSpec((1, T, D), lambda b, t: (b, t, 0)),
            pl.BlockSpec((1, T, 3 * D), lambda b, t: (b, t, 0)),
            tile(lay["xa"]), tile(lay["ba"]), tile(lay["ca"]),
            halo(lay["xa"]), halo(lay["ca"]),
            tile(lay["uc"]), tile(lay["vc"]),
            pl.BlockSpec((1, T, dm), lambda b, t: (b, t, 0)),
            pl.BlockSpec((1, 1, D), lambda b, t: ((l * B + b) * N_ADA + 5, 0, 0)),
            pl.BlockSpec((1, 1, D), lambda b, t: (l * N_NORMS + 3, 0, 0)),
            const((K, dm)), const((1, dm)), const((1, dm)),
            const((GROUPS, SPATIAL_CHUNK, SPATIAL_CHUNK)), const((SPATIAL_CHUNK, dm)),
            const((3, dm, D)), const((D, D)),
        ],
        out_specs=pl.BlockSpec((1, T, D), lambda b, t: (b, t, 0)),
        out_shape=jax.ShapeDtypeStruct((B, S, D), F32),
        scratch_shapes=[pltpu.VMEM((HALO + T, dm), F32), pltpu.VMEM((T, dm), BF16)],
        compiler_params=_params("parallel", "parallel"),
        name="mix",
    )(h, proj, proj, proj, proj, proj, proj, proj, proj, yb, ada, ng,
      conv_a, ln_g, ln_b, w_sp, b_sp, wb, wo)


def _pad_axis(w, axis, mult):
    pad = (-w.shape[axis]) % mult
    widths = [(0, pad) if a == axis else (0, 0) for a in range(w.ndim)]
    return jnp.pad(w, widths) if pad else w


def kernel(x, c, w_ada, b_ada, norm_g, ffn_w_gate, ffn_w_up, ffn_w_down, w_in, conv_a, conv_qkv, a_log,
           dt_bias, gdn_norm_g, ln_v_g, ln_v_b, w_spatial, b_spatial, w_branch, w_o):
    B, S, D = x.shape
    L = w_ada.shape[0]
    dm = D // 2
    assert dm == HEADS * HEAD_DIM and dm == GROUPS * SPATIAL_CHUNK
    lay = _proj_layout(D)

    ada = _ada_call(c, w_ada, b_ada).reshape(L * B * N_ADA, 1, D)
    ng = norm_g.reshape(L * N_NORMS, 1, D)

    wg = _pad_axis(ffn_w_gate, 3, FFN_TF).astype(BF16)
    wu = _pad_axis(ffn_w_up, 3, FFN_TF).astype(BF16)
    wd = _pad_axis(ffn_w_down, 2, FFN_TF).astype(BF16)
    o_qkv, o_z, o_small = 3 * dm, 6 * dm, 7 * dm
    o_uc = o_small + 2 * HEADS
    o_g = o_uc + 2 * dm
    w_main = jnp.concatenate([w_in[..., o_g:], w_in[..., o_qkv:o_z], w_in[..., :o_qkv], w_in[..., o_z:o_small],
                              w_in[..., o_uc:o_g]], axis=-1).astype(BF16)
    w_small = _pad_axis(w_in[..., o_small:o_uc], 2, SMALL).astype(BF16)
    wb = w_branch.astype(BF16)
    wo = w_o.astype(BF16)
    lane_pad = lambda v: jnp.pad(v, ((0, 0), (HEADS, SMALL - 2 * HEADS))).reshape(L, 1, SMALL)
    alog_rows, dt_rows = lane_pad(a_log), lane_pad(dt_bias)
    gdn_rows = gdn_norm_g.reshape(L, 1, HEAD_DIM)
    ln_g, ln_b = ln_v_g.reshape(L, 1, dm), ln_v_b.reshape(L, 1, dm)
    b_exp = jnp.repeat(jnp.swapaxes(b_spatial, 1, 2), dm // GROUPS, axis=2)

    h = x
    for l in range(L):
        h = _ffn_call(h, ada, ng, l, 0, wg, wu, wd)
        proj, small = _proj_call(h, ada, ng, l, w_main, w_small)
        qkv = _prep_call(proj, l, conv_qkv)
        yb = _delta_call(qkv, proj, small, l, alog_rows, dt_rows, gdn_rows, lay["z"])
        h = _mix_call(h, proj, yb, ada, ng, l, conv_a, ln_g, ln_b, w_spatial, b_exp, wb, wo)
        h = _ffn_call(h, ada, ng, l, 1, wg, wu, wd)
    return h
```

```python
import functools

import jax
import jax.numpy as jnp
from jax import lax
from jax.experimental import pallas as pl
from jax.experimental.pallas import tpu as pltpu

F32 = jnp.float32
BF16 = jnp.bfloat16

EPS = 1e-6
HEADS = 8
HEAD_DIM = 128
DELTA_CHUNK = 64
GROUPS = 8
SPATIAL_CHUNK = 128
N_ADA = 9
N_NORMS = 6
HALO = 16
SMALL = 128

VMEM_LIMIT = 62 * 1024 * 1024

FFN_TM, FFN_TF = 1024, 512
PROJ_TM, PROJ_TN = 1024, 1024
PREP_T = 256
DELTA_T = 256
MIX_T = 256
ADA_TN = 1024
ROW_CHUNK = 16


def _sigmoid(x):
    return 1.0 / (1.0 + jnp.exp(-x))


def _silu(x):
    return x * _sigmoid(x)


def _gelu_tanh(x):
    return 0.5 * x * (1.0 + jnp.tanh(0.7978845608028654 * (x + 0.044715 * (x * x * x))))


def _softplus(x):
    return jnp.maximum(x, 0.0) + jnp.log(1.0 + jnp.exp(-jnp.abs(x)))


def _rms(x):
    return x * lax.rsqrt(jnp.mean(x * x, axis=-1, keepdims=True) + EPS)


def _dot(a, b):
    return jnp.dot(a, b, preferred_element_type=F32)


def _for_row_chunks(n_rows, fn, unroll):
    def body(r, carry):
        fn(pl.ds(pl.multiple_of(r * ROW_CHUNK, ROW_CHUNK), ROW_CHUNK))
        return carry
    lax.fori_loop(0, n_rows // ROW_CHUNK, body, 0, unroll=unroll)


def _row_rsqrt_pass(load, rs_scr, n_rows, width):
    lanes = rs_scr.shape[1]

    def partial_sums(rows):
        sq = load(rows)
        sq = sq * sq
        acc = sq[:, 0:lanes]
        for c in range(1, width // lanes):
            acc = acc + sq[:, c * lanes:(c + 1) * lanes]
        rs_scr[rows, :] = acc

    _for_row_chunks(n_rows, partial_sums, unroll=4)
    ms = jnp.sum(rs_scr[...], axis=-1, keepdims=True) / width
    rs_scr[...] = jnp.broadcast_to(lax.rsqrt(ms + EPS), rs_scr.shape)


def _lanes(rs, width):
    return jnp.tile(rs, (1, width // rs.shape[1]))


def _rows(v, width):
    return jnp.broadcast_to(v, (ROW_CHUNK, width))


def _params(*sem):
    return pltpu.CompilerParams(dimension_semantics=sem, vmem_limit_bytes=VMEM_LIMIT)


def _ada_kernel(c_ref, w_ref, b_ref, o_ref):
    c = c_ref[...]
    act = _silu(c).astype(BF16)
    o_ref[0] = _dot(act, w_ref[0].astype(BF16)) + b_ref[0]


def _ada_call(c, w_ada, b_ada):
    L, D, N = w_ada.shape
    B = c.shape[0]
    return pl.pallas_call(
        _ada_kernel,
        grid=(L, N // ADA_TN),
        in_specs=[
            pl.BlockSpec((B, D), lambda l, j: (0, 0)),
            pl.BlockSpec((1, D, ADA_TN), lambda l, j: (l, 0, j)),
            pl.BlockSpec((1, 1, ADA_TN), lambda l, j: (l, 0, j)),
        ],
        out_specs=pl.BlockSpec((1, B, ADA_TN), lambda l, j: (l, 0, j)),
        out_shape=jax.ShapeDtypeStruct((L, B, N), F32),
        compiler_params=_params("parallel", "parallel"),
        name="ada",
    )(c, w_ada, b_ada.reshape(L, 1, N))


def _ffn_kernel(h_ref, sh_ref, sc_ref, gt_ref, gpre_ref, gpost_ref, wg_ref, wu_ref, wd_ref,
                o_ref, n_scr, rs_scr, aff_scr):
    j = pl.program_id(2)
    tm, D = h_ref.shape[1], h_ref.shape[2]

    @pl.when(j == 0)
    def _():
        aff_scr[0] = _rows(gpre_ref[0] * (1.0 + sc_ref[0]), D)
        aff_scr[1] = _rows(sh_ref[0], D)
        _row_rsqrt_pass(lambda rows: h_ref[0, rows, :], rs_scr, tm, D)

        def prologue(rows):
            n = h_ref[0, rows, :] * _lanes(rs_scr[rows, :], D) * aff_scr[0] + aff_scr[1]
            n_scr[rows, :] = n.astype(BF16)

        _for_row_chunks(tm, prologue, unroll=2)

    def hidden_tile():
        n = n_scr[...]
        gate = _dot(n, wg_ref[...])
        up = _dot(n, wu_ref[...])
        return (_silu(gate) * up).astype(BF16)

    @pl.when(j == 0)
    def _():
        o_ref[0] = _dot(hidden_tile(), wd_ref[...])

    @pl.when(j > 0)
    def _():
        o_ref[0] += _dot(hidden_tile(), wd_ref[...])

    @pl.when(j == pl.num_programs(2) - 1)
    def _():
        aff_scr[0] = _rows((0.5 * gt_ref[0]) * gpost_ref[0], D)
        _row_rsqrt_pass(lambda rows: o_ref[0, rows, :], rs_scr, tm, D)

        def epilogue(rows):
            r = o_ref[0, rows, :] * _lanes(rs_scr[rows, :], D)
            o_ref[0, rows, :] = h_ref[0, rows, :] + r * aff_scr[0]

        _for_row_chunks(tm, epilogue, unroll=2)


def _ffn_call(h, ada, ng, l, s, wg, wu, wd):
    B, S, D = h.shape
    Fp = wg.shape[-1]
    tm, tf = min(FFN_TM, S), FFN_TF
    k0 = 6 * s
    ada_spec = lambda k: pl.BlockSpec((1, 1, D), lambda b, i, j: ((l * B + b) * N_ADA + k, 0, 0))
    ng_spec = lambda k: pl.BlockSpec((1, 1, D), lambda b, i, j: (l * N_NORMS + k, 0, 0))
    npre = 4 * s
    return pl.pallas_call(
        _ffn_kernel,
        grid=(B, S // tm, Fp // tf),
        in_specs=[
            pl.BlockSpec((1, tm, D), lambda b, i, j: (b, i, 0)),
            ada_spec(k0), ada_spec(k0 + 1), ada_spec(k0 + 2),
            ng_spec(npre), ng_spec(npre + 1),
            pl.BlockSpec((None, None, D, tf), lambda b, i, j: (l, s, 0, j)),
            pl.BlockSpec((None, None, D, tf), lambda b, i, j: (l, s, 0, j)),
            pl.BlockSpec((None, None, tf, D), lambda b, i, j: (l, s, j, 0)),
        ],
        out_specs=pl.BlockSpec((1, tm, D), lambda b, i, j: (b, i, 0)),
        out_shape=jax.ShapeDtypeStruct((B, S, D), F32),
        scratch_shapes=[pltpu.VMEM((tm, D), BF16), pltpu.VMEM((tm, 128), F32),
                        pltpu.VMEM((2, ROW_CHUNK, D), F32)],
        compiler_params=_params("parallel", "parallel", "arbitrary"),
        name="ffn",
    )(h, ada, ada, ada, ng, ng, wg, wu, wd)


def _proj_kernel(h_ref, sh_ref, sc_ref, g_ref, w_ref, ws_ref, o_ref, os_ref, n_scr):
    j = pl.program_id(2)

    @pl.when(j == 0)
    def _():
        n = (_rms(h_ref[0]) * g_ref[0] * (1.0 + sc_ref[0]) + sh_ref[0]).astype(BF16)
        n_scr[...] = n
        os_ref[0] = _dot(n, ws_ref[...])

    o_ref[0] = _dot(n_scr[...], w_ref[...]).astype(BF16)


def _proj_call(h, ada, ng, l, w_main, w_small):
    B, S, D = h.shape
    N = w_main.shape[-1]
    tm, tn = min(PROJ_TM, S), PROJ_TN
    ada_spec = lambda k: pl.BlockSpec((1, 1, D), lambda b, i, j: ((l * B + b) * N_ADA + k, 0, 0))
    return pl.pallas_call(
        _proj_kernel,
        grid=(B, S // tm, N // tn),
        in_specs=[
            pl.BlockSpec((1, tm, D), lambda b, i, j: (b, i, 0)),
            ada_spec(3), ada_spec(4),
            pl.BlockSpec((1, 1, D), lambda b, i, j: (l * N_NORMS + 2, 0, 0)),
            pl.BlockSpec((None, D, tn), lambda b, i, j: (l, 0, j)),
            pl.BlockSpec((None, D, SMALL), lambda b, i, j: (l, 0, 0)),
        ],
        out_specs=[
            pl.BlockSpec((1, tm, tn), lambda b, i, j: (b, i, j)),
            pl.BlockSpec((1, tm, SMALL), lambda b, i, j: (b, i, 0)),
        ],
        out_shape=[jax.ShapeDtypeStruct((B, S, N), BF16), jax.ShapeDtypeStruct((B, S, SMALL), F32)],
        scratch_shapes=[pltpu.VMEM((tm, D), BF16)],
        compiler_params=_params("parallel", "parallel", "arbitrary"),
        name="proj",
    )(h, ada, ada, ng, w_main, w_small)


def _proj_layout(D):
    dm = D // 2
    gates = 0
    qkv = 3 * D
    rest = qkv + 3 * dm
    blk = rest // dm
    return dict(xa=blk, ba=blk + 1, ca=blk + 2, z=blk + 3, uc=blk + 4, vc=blk + 5)


def _prep_kernel(x_ref, halo_ref, w_ref, o_ref, ext_scr):
    t = pl.program_id(1)
    T = x_ref.shape[1]
    dm = x_ref.shape[2] // 3
    x = x_ref[0].astype(F32)
    halo = jnp.where(t > 0, halo_ref[0].astype(F32), 0.0)
    ext_scr[0:HALO, :] = halo
    ext_scr[HALO:HALO + T, :] = x
    K = w_ref.shape[0]
    acc = w_ref[K - 1:K, :] * x
    for s in range(1, K):
        acc = acc + w_ref[K - 1 - s:K - s, :] * ext_scr[HALO - s:HALO - s + T, :]
    y = _silu(acc)
    for i in range(2 * HEADS):
        sl = slice(i * HEAD_DIM, (i + 1) * HEAD_DIM)
        v = y[:, sl]
        v = v * lax.rsqrt(jnp.sum(v * v, axis=-1, keepdims=True) + EPS)
        if i < HEADS:
            v = v * (HEAD_DIM ** -0.5)
        o_ref[0, :, sl] = v.astype(BF16)
    o_ref[0, :, 2 * dm:3 * dm] = y[:, 2 * dm:3 * dm].astype(BF16)


def _prep_call(proj, l, conv_w):
    B, S, _ = proj.shape
    _, K, W = conv_w.shape
    T = min(PREP_T, S)
    rb = T // HALO
    return pl.pallas_call(
        _prep_kernel,
        grid=(B, S // T),
        in_specs=[
            pl.BlockSpec((1, T, W), lambda b, t: (b, t, 2)),
            pl.BlockSpec((1, HALO, W), lambda b, t: (b, jnp.maximum(t * rb - 1, 0), 2)),
            pl.BlockSpec((None, K, W), lambda b, t: (l, 0, 0)),
        ],
        out_specs=pl.BlockSpec((1, T, W), lambda b, t: (b, t, 0)),
        out_shape=jax.ShapeDtypeStruct((B, S, W), BF16),
        scratch_shapes=[pltpu.VMEM((HALO + T, W), F32)],
        compiler_params=_params("parallel", "parallel"),
        name="qkv_prep",
    )(proj, proj, conv_w)


def _delta_kernel(qkv_ref, z_ref, sm_ref, alog_ref, dt_ref, ng_ref, o_ref, s_scr):
    t = pl.program_id(1)
    T = qkv_ref.shape[1]
    dm = HEADS * HEAD_DIM
    C = DELTA_CHUNK
    nchunk = T // C
    heads = range(HEADS)
    bf = lambda a: a.astype(BF16)

    @pl.when(t == 0)
    def _():
        s_scr[...] = jnp.zeros_like(s_scr)

    sm = sm_ref[0]
    beta_all = _sigmoid(sm)
    g_all = -jnp.exp(alog_ref[...]) * _softplus(sm + dt_ref[...])

    ri = lax.broadcasted_iota(jnp.int32, (T, T), 0)
    ci = lax.broadcasted_iota(jnp.int32, (T, T), 1)
    same_block = lambda size: jnp.right_shift(ri, size.bit_length() - 1) == jnp.right_shift(ci, size.bit_length() - 1)
    chunk = same_block(C)
    incl = chunk & (ri >= ci)
    strict = chunk & (ri > ci)
    eye = jnp.where(ri == ci, 1.0, 0.0)
    tri = bf(jnp.where(incl, 1.0, 0.0))

    g_hi = bf(g_all)
    r1 = g_all - g_hi.astype(F32)
    g_mid = bf(r1)
    g_lo = bf(r1 - g_mid.astype(F32))
    gcum = _dot(tri, g_hi) + _dot(tri, g_mid) + _dot(tri, g_lo)
    glast = jnp.concatenate(
        [jnp.broadcast_to(gcum[c * C + C - 1:c * C + C, :], (C, SMALL)) for c in range(nchunk)], axis=0)
    eg_all = jnp.exp(gcum)
    kdec_all = jnp.exp(glast - gcum)
    egl_all = jnp.exp(glast)
    gcum_t = gcum.T

    col = lambda a, h: a[:, HEADS + h:HEADS + h + 1]
    kf = [qkv_ref[0, :, dm + h * HEAD_DIM:dm + (h + 1) * HEAD_DIM] for h in heads]
    qf = [qkv_ref[0, :, h * HEAD_DIM:(h + 1) * HEAD_DIM].astype(F32) for h in heads]
    kb = [kf[h].astype(F32) * beta_all[:, h:h + 1] for h in heads]
    sc = [lax.dot_general(bf(jnp.concatenate([kb[h], qf[h]], axis=0)), kf[h], (((1,), (1,)), ((), ())),
                          preferred_element_type=F32) for h in heads]
    dec = [jnp.exp(jnp.where(incl, col(gcum, h) - gcum_t[HEADS + h:HEADS + h + 1, :], -jnp.inf)) for h in heads]
    m = [jnp.where(strict, sc[h][:T] * dec[h], 0.0) for h in heads]
    attn = [jnp.where(incl, sc[h][T:] * dec[h], 0.0) for h in heads]
    rhs = [bf(jnp.concatenate(
        [qkv_ref[0, :, 2 * dm + h * HEAD_DIM:2 * dm + (h + 1) * HEAD_DIM].astype(F32) * beta_all[:, h:h + 1],
         kb[h] * col(eg_all, h)], axis=1)) for h in heads]

    base = 8
    inner = same_block(base)
    p = [bf(jnp.where(inner, m[h], 0.0)) for h in heads]
    inv = [eye - jnp.where(inner, m[h], 0.0) for h in heads]
    size = 2
    while size < base:
        p = [bf(_dot(p[h], p[h])) for h in heads]
        inv = [inv[h] + _dot(bf(inv[h]), p[h]) for h in heads]
        size *= 2
    size = base
    while size < C:
        outer = same_block(2 * size)
        off = [bf(jnp.where(outer & jnp.logical_not(inner), m[h], 0.0)) for h in heads]
        invb = [bf(inv[h]) for h in heads]
        half = [bf(_dot(invb[h], off[h])) for h in heads]
        inv = [inv[h] - _dot(half[h], invb[h]) for h in heads]
        inner = outer
        size *= 2
    x = [_dot(bf(inv[h]), rhs[h]) for h in heads]

    qg = [qf[h] * col(eg_all, h) for h in heads]
    kd_t = [(kf[h].astype(F32) * col(kdec_all, h)).T for h in heads]
    state = [s_scr[h] for h in heads]
    outs = [[] for _ in heads]
    for c in range(nchunk):
        rows = slice(c * C, (c + 1) * C)
        a = [bf(jnp.concatenate([x[h][rows, HEAD_DIM:], qg[h][rows]], axis=0)) for h in heads]
        r = [_dot(a[h], bf(state[h])) for h in heads]
        v_new = [x[h][rows, :HEAD_DIM] - r[h][:C] for h in heads]
        pieces = lambda vn: ([jnp.zeros((c * C, HEAD_DIM), F32)] if c > 0 else []) + [vn] + (
            [jnp.zeros((T - (c + 1) * C, HEAD_DIM), F32)] if c < nchunk - 1 else [])
        v_pad = [bf(jnp.concatenate(pieces(v_new[h]), axis=0)) for h in heads]
        lb = [bf(jnp.concatenate([attn[h][rows], kd_t[h]], axis=0)) for h in heads]
        r2 = [_dot(lb[h], v_pad[h]) for h in heads]
        for h in heads:
            outs[h].append(r[h][C:] + r2[h][:C])
        state = [state[h] * egl_all[c * C:c * C + 1, HEADS + h:HEADS + h + 1] + r2[h][C:] for h in heads]
    for h in heads:
        hs = slice(h * HEAD_DIM, (h + 1) * HEAD_DIM)
        s_scr[h] = state[h]
        o = jnp.concatenate(outs[h], axis=0)
        zf = z_ref[0, :, hs].astype(F32)
        o_ref[0, :, hs] = bf(_rms(o) * ng_ref[...] * _silu(zf))


def _delta_call(qkv, proj, small, l, alog_row, dt_row, gdn_g, z_blk):
    B, S, W = qkv.shape
    dm = W // 3
    T = min(DELTA_T, S)
    return pl.pallas_call(
        _delta_kernel,
        grid=(B, S // T),
        in_specs=[
            pl.BlockSpec((1, T, W), lambda b, t: (b, t, 0)),
            pl.BlockSpec((1, T, dm), lambda b, t: (b, t, z_blk)),
            pl.BlockSpec((1, T, SMALL), lambda b, t: (b, t, 0)),
            pl.BlockSpec((None, 1, SMALL), lambda b, t: (l, 0, 0)),
            pl.BlockSpec((None, 1, SMALL), lambda b, t: (l, 0, 0)),
            pl.BlockSpec((None, 1, HEAD_DIM), lambda b, t: (l, 0, 0)),
        ],
        out_specs=pl.BlockSpec((1, T, dm), lambda b, t: (b, t, 0)),
        out_shape=jax.ShapeDtypeStruct((B, S, dm), BF16),
        scratch_shapes=[pltpu.VMEM((HEADS, HEAD_DIM, HEAD_DIM), F32)],
        compiler_params=_params("parallel", "arbitrary"),
        name="delta",
    )(qkv, proj, small, alog_row, dt_row, gdn_g)


def _mix_kernel(h_ref, gates_ref, xa_ref, ba_ref, ca_ref, xah_ref, cah_ref, uc_ref, vc_ref, yb_ref,
                gt_ref, ng_ref, cw_ref, lng_ref, lnb_ref, wsp_ref, bsp_ref, wb_ref, wo_ref,
                o_ref, ext_scr, yc_scr):
    t = pl.program_id(1)
    T = h_ref.shape[1]
    D = h_ref.shape[2]
    SC = SPATIAL_CHUNK
    nchunk = T // SC

    p = ca_ref[0].astype(F32) * xa_ref[0].astype(F32)
    ph = jnp.where(t > 0, cah_ref[0].astype(F32) * xah_ref[0].astype(F32), 0.0)
    ext_scr[0:HALO, :] = ph
    ext_scr[HALO:HALO + T, :] = p
    K = cw_ref.shape[0]
    conv = cw_ref[K - 1:K, :] * p
    for s in range(1, K):
        conv = conv + cw_ref[K - 1 - s:K - s, :] * ext_scr[HALO - s:HALO - s + T, :]
    y_a = (ba_ref[0].astype(F32) * conv).astype(BF16)

    v = _gelu_tanh(vc_ref[0].astype(F32))
    vcen = v - jnp.mean(v, axis=-1, keepdims=True)
    vn = vcen * lax.rsqrt(jnp.mean(vcen * vcen, axis=-1, keepdims=True) + EPS)
    vn = (vn * lng_ref[...] + lnb_ref[...]).astype(BF16)
    ri = lax.broadcasted_iota(jnp.int32, (SC, SC), 0)
    ci = lax.broadcasted_iota(jnp.int32, (SC, SC), 1)
    causal = ri >= ci
    gw = SC
    for g in range(GROUPS):
        gs = slice(g * gw, (g + 1) * gw)
        wg = jnp.where(causal, wsp_ref[g], 0.0).astype(BF16)
        rhs = jnp.concatenate([vn[c * SC:(c + 1) * SC, gs] for c in range(nchunk)], axis=1)
        mx = _dot(wg, rhs)
        mixed = jnp.concatenate([mx[:, c * gw:(c + 1) * gw] for c in range(nchunk)], axis=0)
        bias = jnp.concatenate([bsp_ref[:, gs]] * nchunk, axis=0)
        u = _gelu_tanh(uc_ref[0, :, gs].astype(F32))
        yc_scr[:, gs] = (u * (mixed + bias)).astype(BF16)

    gates = gates_ref[0]
    merged = _sigmoid(gates[:, 0:D].astype(F32)) * _dot(y_a, wb_ref[0])
    merged = merged + _sigmoid(gates[:, D:2 * D].astype(F32)) * _dot(yb_ref[0], wb_ref[1])
    merged = merged + _sigmoid(gates[:, 2 * D:3 * D].astype(F32)) * _dot(yc_scr[...], wb_ref[2])
    y = _dot(merged.astype(BF16), wo_ref[...])
    o_ref[0] = h_ref[0] + gt_ref[0] * (_rms(y) * ng_ref[0])


def _mix_call(h, proj, yb, ada, ng, l, conv_a, ln_g, ln_b, w_sp, b_sp, wb, wo):
    B, S, D = h.shape
    dm = D // 2
    lay = _proj_layout(D)
    T = min(MIX_T, S)
    rb = T // HALO
    K = conv_a.shape[1]
    one = pl.Buffered(1)
    tile = lambda blk: pl.BlockSpec((1, T, dm), lambda b, t: (b, t, blk))
    halo = lambda blk: pl.BlockSpec((1, HALO, dm), lambda b, t: (b, jnp.maximum(t * rb - 1, 0), blk))
    const = lambda shape: pl.BlockSpec((None,) + shape, lambda b, t: (l,) + (0,) * len(shape), pipeline_mode=one)
    return pl.pallas_call(
        _mix_kernel,
        grid=(B, S // T),
        in_specs=[
            pl.BlockSpec((1, T, D), lambda b, t: (b, t, 0)),
            pl.BlockSpec((1, T, 3 * D), lambda b, t: (b, t, 0)),
            tile(lay["xa"]), tile(lay["ba"]), tile(lay["ca"]),
            halo(lay["xa"]), halo(lay["ca"]),
            tile(lay["uc"]), tile(lay["vc"]),
            pl.BlockSpec((1, T, dm), lambda b, t: (b, t, 0)),
            pl.BlockSpec((1, 1, D), lambda b, t: ((l * B + b) * N_ADA + 5, 0, 0)),
            pl.BlockSpec((1, 1, D), lambda b, t: (l * N_NORMS + 3, 0, 0)),
            const((K, dm)), const((1, dm)), const((1, dm)),
            const((GROUPS, SPATIAL_CHUNK, SPATIAL_CHUNK)), const((SPATIAL_CHUNK, dm)),
            const((3, dm, D)), const((D, D)),
        ],
        out_specs=pl.BlockSpec((1, T, D), lambda b, t: (b, t, 0)),
        out_shape=jax.ShapeDtypeStruct((B, S, D), F32),
        scratch_shapes=[pltpu.VMEM((HALO + T, dm), F32), pltpu.VMEM((T, dm), BF16)],
        compiler_params=_params("parallel", "parallel"),
        name="mix",
    )(h, proj, proj, proj, proj, proj, proj, proj, proj, yb, ada, ng,
      conv_a, ln_g, ln_b, w_sp, b_sp, wb, wo)


def _pad_axis(w, axis, mult):
    pad = (-w.shape[axis]) % mult
    if not pad:
        return w
    zeros = jnp.zeros(w.shape[:axis] + (pad,) + w.shape[axis + 1:], w.dtype)
    return jnp.concatenate([w, zeros], axis=axis)


def kernel(x, c, w_ada, b_ada, norm_g, ffn_w_gate, ffn_w_up, ffn_w_down, w_in, conv_a, conv_qkv, a_log,
           dt_bias, gdn_norm_g, ln_v_g, ln_v_b, w_spatial, b_spatial, w_branch, w_o):
    B, S, D = x.shape
    L = w_ada.shape[0]
    dm = D // 2
    assert dm == HEADS * HEAD_DIM and dm == GROUPS * SPATIAL_CHUNK
    lay = _proj_layout(D)

    ada = _ada_call(c, w_ada, b_ada).reshape(L * B * N_ADA, 1, D)
    ng = norm_g.reshape(L * N_NORMS, 1, D)

    wg = _pad_axis(ffn_w_gate.astype(BF16), 3, FFN_TF)
    wu = _pad_axis(ffn_w_up.astype(BF16), 3, FFN_TF)
    wd = _pad_axis(ffn_w_down.astype(BF16), 2, FFN_TF)
    o_qkv, o_z, o_small = 3 * dm, 6 * dm, 7 * dm
    o_uc = o_small + 2 * HEADS
    o_g = o_uc + 2 * dm
    w_main = jnp.concatenate([w_in[..., o_g:], w_in[..., o_qkv:o_z], w_in[..., :o_qkv], w_in[..., o_z:o_small],
                              w_in[..., o_uc:o_g]], axis=-1).astype(BF16)
    w_small = _pad_axis(w_in[..., o_small:o_uc], 2, SMALL).astype(BF16)
    wb = w_branch.astype(BF16)
    wo = w_o.astype(BF16)
    lane_pad = lambda v: jnp.pad(v, ((0, 0), (HEADS, SMALL - 2 * HEADS))).reshape(L, 1, SMALL)
    alog_rows, dt_rows = lane_pad(a_log), lane_pad(dt_bias)
    gdn_rows = gdn_norm_g.reshape(L, 1, HEAD_DIM)
    ln_g, ln_b = ln_v_g.reshape(L, 1, dm), ln_v_b.reshape(L, 1, dm)
    b_exp = jnp.repeat(jnp.swapaxes(b_spatial, 1, 2), dm // GROUPS, axis=2)

    h = x
    for l in range(L):
        h = _ffn_call(h, ada, ng, l, 0, wg, wu, wd)
        proj, small = _proj_call(h, ada, ng, l, w_main, w_small)
        qkv = _prep_call(proj, l, conv_qkv)
        yb = _delta_call(qkv, proj, small, l, alog_rows, dt_rows, gdn_rows, lay["z"])
        h = _mix_call(h, proj, yb, ada, ng, l, conv_a, ln_g, ln_b, w_spatial, b_exp, wb, wo)
        h = _ffn_call(h, ada, ng, l, 1, wg, wu, wd)
    return h
```

```python
import functools

import jax
import jax.numpy as jnp
from jax import lax
from jax.experimental import pallas as pl
from jax.experimental.pallas import tpu as pltpu

F32 = jnp.float32
BF16 = jnp.bfloat16

EPS = 1e-6
HEADS = 8
HEAD_DIM = 128
DELTA_CHUNK = 64
GROUPS = 8
SPATIAL_CHUNK = 128
N_ADA = 9
N_NORMS = 6
HALO = 16
SMALL = 128

VMEM_LIMIT = 62 * 1024 * 1024

FFN_TM, FFN_TF = 1024, 512
PROJ_TM, PROJ_TN = 1024, 1024
PREP_T = 256
DELTA_T = 256
MIX_T = 256
ADA_TN = 1024
ROW_CHUNK = 16


def _sigmoid(x):
    return 1.0 / (1.0 + jnp.exp(-x))


def _silu(x):
    return x * _sigmoid(x)


def _gelu_tanh(x):
    return 0.5 * x * (1.0 + jnp.tanh(0.7978845608028654 * (x + 0.044715 * (x * x * x))))


def _softplus(x):
    return jnp.maximum(x, 0.0) + jnp.log(1.0 + jnp.exp(-jnp.abs(x)))


def _rms(x):
    return x * lax.rsqrt(jnp.mean(x * x, axis=-1, keepdims=True) + EPS)


def _dot(a, b):
    return jnp.dot(a, b, preferred_element_type=F32)


def _for_row_chunks(n_rows, fn, unroll):
    def body(r, carry):
        fn(pl.ds(pl.multiple_of(r * ROW_CHUNK, ROW_CHUNK), ROW_CHUNK))
        return carry
    lax.fori_loop(0, n_rows // ROW_CHUNK, body, 0, unroll=unroll)


def _row_rsqrt_pass(load, rs_scr, n_rows, width):
    lanes = rs_scr.shape[1]

    def partial_sums(rows):
        sq = load(rows)
        sq = sq * sq
        acc = sq[:, 0:lanes]
        for c in range(1, width // lanes):
            acc = acc + sq[:, c * lanes:(c + 1) * lanes]
        rs_scr[rows, :] = acc

    _for_row_chunks(n_rows, partial_sums, unroll=4)
    ms = jnp.sum(rs_scr[...], axis=-1, keepdims=True) / width
    rs_scr[...] = jnp.broadcast_to(lax.rsqrt(ms + EPS), rs_scr.shape)


def _lanes(rs, width):
    return jnp.tile(rs, (1, width // rs.shape[1]))


def _rows(v, width):
    return jnp.broadcast_to(v, (ROW_CHUNK, width))


def _params(*sem):
    return pltpu.CompilerParams(dimension_semantics=sem, vmem_limit_bytes=VMEM_LIMIT)


def _ada_kernel(c_ref, w_ref, b_ref, o_ref):
    c = c_ref[...]
    act = _silu(c).astype(BF16)
    o_ref[0] = _dot(act, w_ref[0].astype(BF16)) + b_ref[0]


def _ada_call(c, w_ada, b_ada):
    L, D, N = w_ada.shape
    B = c.shape[0]
    return pl.pallas_call(
        _ada_kernel,
        grid=(L, N // ADA_TN),
        in_specs=[
            pl.BlockSpec((B, D), lambda l, j: (0, 0)),
            pl.BlockSpec((1, D, ADA_TN), lambda l, j: (l, 0, j)),
            pl.BlockSpec((1, 1, ADA_TN), lambda l, j: (l, 0, j)),
        ],
        out_specs=pl.BlockSpec((1, B, ADA_TN), lambda l, j: (l, 0, j)),
        out_shape=jax.ShapeDtypeStruct((L, B, N), F32),
        compiler_params=_params("parallel", "parallel"),
        name="ada",
    )(c, w_ada, b_ada.reshape(L, 1, N))


def _ffn_kernel(h_ref, sh_ref, sc_ref, gt_ref, gpre_ref, gpost_ref, wg_ref, wu_ref, wd_ref,
                o_ref, n_scr, rs_scr, aff_scr):
    j = pl.program_id(2)
    tm, D = h_ref.shape[1], h_ref.shape[2]

    @pl.when(j == 0)
    def _():
        aff_scr[0] = _rows(gpre_ref[0] * (1.0 + sc_ref[0]), D)
        aff_scr[1] = _rows(sh_ref[0], D)
        _row_rsqrt_pass(lambda rows: h_ref[0, rows, :], rs_scr, tm, D)

        def prologue(rows):
            n = h_ref[0, rows, :] * _lanes(rs_scr[rows, :], D) * aff_scr[0] + aff_scr[1]
            n_scr[rows, :] = n.astype(BF16)

        _for_row_chunks(tm, prologue, unroll=2)

    def hidden_tile():
        n = n_scr[...]
        gate = _dot(n, wg_ref[...])
        up = _dot(n, wu_ref[...])
        return (_silu(gate) * up).astype(BF16)

    @pl.when(j == 0)
    def _():
        o_ref[0] = _dot(hidden_tile(), wd_ref[...])

    @pl.when(j > 0)
    def _():
        o_ref[0] += _dot(hidden_tile(), wd_ref[...])

    @pl.when(j == pl.num_programs(2) - 1)
    def _():
        aff_scr[0] = _rows((0.5 * gt_ref[0]) * gpost_ref[0], D)
        _row_rsqrt_pass(lambda rows: o_ref[0, rows, :], rs_scr, tm, D)

        def epilogue(rows):
            r = o_ref[0, rows, :] * _lanes(rs_scr[rows, :], D)
            o_ref[0, rows, :] = h_ref[0, rows, :] + r * aff_scr[0]

        _for_row_chunks(tm, epilogue, unroll=2)


def _ffn_call(h, ada, ng, l, s, wg, wu, wd):
    B, S, D = h.shape
    Fp = wg.shape[-1]
    tm, tf = min(FFN_TM, S), FFN_TF
    k0 = 6 * s
    ada_spec = lambda k: pl.BlockSpec((1, 1, D), lambda b, i, j: ((l * B + b) * N_ADA + k, 0, 0))
    ng_spec = lambda k: pl.BlockSpec((1, 1, D), lambda b, i, j: (l * N_NORMS + k, 0, 0))
    npre = 4 * s
    return pl.pallas_call(
        _ffn_kernel,
        grid=(B, S // tm, Fp // tf),
        in_specs=[
            pl.BlockSpec((1, tm, D), lambda b, i, j: (b, i, 0)),
            ada_spec(k0), ada_spec(k0 + 1), ada_spec(k0 + 2),
            ng_spec(npre), ng_spec(npre + 1),
            pl.BlockSpec((None, None, D, tf), lambda b, i, j: (l, s, 0, j)),
            pl.BlockSpec((None, None, D, tf), lambda b, i, j: (l, s, 0, j)),
            pl.BlockSpec((None, None, tf, D), lambda b, i, j: (l, s, j, 0)),
        ],
        out_specs=pl.BlockSpec((1, tm, D), lambda b, i, j: (b, i, 0)),
        out_shape=jax.ShapeDtypeStruct((B, S, D), F32),
        scratch_shapes=[pltpu.VMEM((tm, D), BF16), pltpu.VMEM((tm, 128), F32),
                        pltpu.VMEM((2, ROW_CHUNK, D), F32)],
        compiler_params=_params("parallel", "parallel", "arbitrary"),
        name="ffn",
    )(h, ada, ada, ada, ng, ng, wg, wu, wd)


def _proj_kernel(h_ref, sh_ref, sc_ref, g_ref, w_ref, ws_ref, o_ref, os_ref, n_scr):
    j = pl.program_id(2)

    @pl.when(j == 0)
    def _():
        n = (_rms(h_ref[0]) * g_ref[0] * (1.0 + sc_ref[0]) + sh_ref[0]).astype(BF16)
        n_scr[...] = n
        os_ref[0] = _dot(n, ws_ref[...])

    o_ref[0] = _dot(n_scr[...], w_ref[...]).astype(BF16)


def _proj_call(h, ada, ng, l, w_main, w_small):
    B, S, D = h.shape
    N = w_main.shape[-1]
    tm, tn = min(PROJ_TM, S), PROJ_TN
    ada_spec = lambda k: pl.BlockSpec((1, 1, D), lambda b, i, j: ((l * B + b) * N_ADA + k, 0, 0))
    return pl.pallas_call(
        _proj_kernel,
        grid=(B, S // tm, N // tn),
        in_specs=[
            pl.BlockSpec((1, tm, D), lambda b, i, j: (b, i, 0)),
            ada_spec(3), ada_spec(4),
            pl.BlockSpec((1, 1, D), lambda b, i, j: (l * N_NORMS + 2, 0, 0)),
            pl.BlockSpec((None, D, tn), lambda b, i, j: (l, 0, j)),
            pl.BlockSpec((None, D, SMALL), lambda b, i, j: (l, 0, 0)),
        ],
        out_specs=[
            pl.BlockSpec((1, tm, tn), lambda b, i, j: (b, i, j)),
            pl.BlockSpec((1, tm, SMALL), lambda b, i, j: (b, i, 0)),
        ],
        out_shape=[jax.ShapeDtypeStruct((B, S, N), BF16), jax.ShapeDtypeStruct((B, S, SMALL), F32)],
        scratch_shapes=[pltpu.VMEM((tm, D), BF16)],
        compiler_params=_params("parallel", "parallel", "arbitrary"),
        name="proj",
    )(h, ada, ada, ng, w_main, w_small)


def _proj_layout(D):
    dm = D // 2
    gates = 0
    qkv = 3 * D
    rest = qkv + 3 * dm
    blk = rest // dm
    return dict(xa=blk, ba=blk + 1, ca=blk + 2, z=blk + 3, uc=blk + 4, vc=blk + 5)


def _prep_kernel(x_ref, halo_ref, w_ref, o_ref, ext_scr):
    t = pl.program_id(1)
    T = x_ref.shape[1]
    W = x_ref.shape[2]
    dm = W // 3
    K = w_ref.shape[0]
    SUB = 8
    xb = x_ref[0]
    x = xb.astype(F32)

    ri = lax.broadcasted_iota(jnp.int32, ((K - 1) * T, T), 0)
    ci = lax.broadcasted_iota(jnp.int32, ((K - 1) * T, T), 1)
    lag = jnp.right_shift(ri, T.bit_length() - 1) + 1
    shift = jnp.where((ri & (T - 1)) - ci == lag, 1.0, 0.0).astype(BF16)
    shifted = _dot(shift, xb)
    acc = w_ref[K - 1:K, :] * x
    for s in range(1, K):
        acc = acc + w_ref[K - 1 - s:K - s, :] * shifted[(s - 1) * T:s * T]

    ext_scr[0:HALO, :] = jnp.where(t > 0, halo_ref[0].astype(F32), 0.0)
    ext_scr[HALO:HALO + SUB, :] = jnp.zeros((SUB, W), F32)
    row = lax.broadcasted_iota(jnp.int32, (SUB, W), 0)
    head = acc[0:SUB]
    for s in range(1, K):
        prev = ext_scr[HALO - s:HALO - s + SUB, :]
        head = head + w_ref[K - 1 - s:K - s, :] * jnp.where(row < s, prev, 0.0)
    y = _silu(jnp.concatenate([head, acc[SUB:]], axis=0))
    for i in range(2 * HEADS):
        sl = slice(i * HEAD_DIM, (i + 1) * HEAD_DIM)
        v = y[:, sl]
        v = v * lax.rsqrt(jnp.sum(v * v, axis=-1, keepdims=True) + EPS)
        if i < HEADS:
            v = v * (HEAD_DIM ** -0.5)
        o_ref[0, :, sl] = v.astype(BF16)
    o_ref[0, :, 2 * dm:3 * dm] = y[:, 2 * dm:3 * dm].astype(BF16)


def _prep_call(proj, l, conv_w):
    B, S, _ = proj.shape
    _, K, W = conv_w.shape
    T = min(PREP_T, S)
    rb = T // HALO
    return pl.pallas_call(
        _prep_kernel,
        grid=(B, S // T),
        in_specs=[
            pl.BlockSpec((1, T, W), lambda b, t: (b, t, 2)),
            pl.BlockSpec((1, HALO, W), lambda b, t: (b, jnp.maximum(t * rb - 1, 0), 2)),
            pl.BlockSpec((None, K, W), lambda b, t: (l, 0, 0)),
        ],
        out_specs=pl.BlockSpec((1, T, W), lambda b, t: (b, t, 0)),
        out_shape=jax.ShapeDtypeStruct((B, S, W), BF16),
        scratch_shapes=[pltpu.VMEM((HALO + 8, W), F32)],
        compiler_params=_params("parallel", "parallel"),
        name="qkv_prep",
    )(proj, proj, conv_w)


def _delta_kernel(qkv_ref, z_ref, sm_ref, alog_ref, dt_ref, ng_ref, o_ref, s_scr):
    t = pl.program_id(1)
    T = qkv_ref.shape[1]
    dm = HEADS * HEAD_DIM
    C = DELTA_CHUNK
    nchunk = T // C
    heads = range(HEADS)
    bf = lambda a: a.astype(BF16)

    @pl.when(t == 0)
    def _():
        s_scr[...] = jnp.zeros_like(s_scr)

    sm = sm_ref[0]
    beta_all = _sigmoid(sm)
    g_all = -jnp.exp(alog_ref[...]) * _softplus(sm + dt_ref[...])

    ri = lax.broadcasted_iota(jnp.int32, (T, T), 0)
    ci = lax.broadcasted_iota(jnp.int32, (T, T), 1)
    same_block = lambda size: jnp.right_shift(ri, size.bit_length() - 1) == jnp.right_shift(ci, size.bit_length() - 1)
    chunk = same_block(C)
    incl = chunk & (ri >= ci)
    strict = chunk & (ri > ci)
    eye = jnp.where(ri == ci, 1.0, 0.0)
    tri = bf(jnp.where(incl, 1.0, 0.0))

    g_hi = bf(g_all)
    r1 = g_all - g_hi.astype(F32)
    g_mid = bf(r1)
    g_lo = bf(r1 - g_mid.astype(F32))
    gcum = _dot(tri, g_hi) + _dot(tri, g_mid) + _dot(tri, g_lo)
    glast = jnp.concatenate(
        [jnp.broadcast_to(gcum[c * C + C - 1:c * C + C, :], (C, SMALL)) for c in range(nchunk)], axis=0)
    eg_all = jnp.exp(gcum)
    kdec_all = jnp.exp(glast - gcum)
    egl_all = jnp.exp(glast)
    gcum_t = gcum.T

    col = lambda a, h: a[:, HEADS + h:HEADS + h + 1]
    kf = [qkv_ref[0, :, dm + h * HEAD_DIM:dm + (h + 1) * HEAD_DIM] for h in heads]
    qf = [qkv_ref[0, :, h * HEAD_DIM:(h + 1) * HEAD_DIM].astype(F32) for h in heads]
    kb = [kf[h].astype(F32) * beta_all[:, h:h + 1] for h in heads]
    sc = [lax.dot_general(bf(jnp.concatenate([kb[h], qf[h]], axis=0)), kf[h], (((1,), (1,)), ((), ())),
                          preferred_element_type=F32) for h in heads]
    dec = [jnp.exp(jnp.where(incl, col(gcum, h) - gcum_t[HEADS + h:HEADS + h + 1, :], -jnp.inf)) for h in heads]
    m = [jnp.where(strict, sc[h][:T] * dec[h], 0.0) for h in heads]
    attn = [jnp.where(incl, sc[h][T:] * dec[h], 0.0) for h in heads]
    rhs = [bf(jnp.concatenate(
        [qkv_ref[0, :, 2 * dm + h * HEAD_DIM:2 * dm + (h + 1) * HEAD_DIM].astype(F32) * beta_all[:, h:h + 1],
         kb[h] * col(eg_all, h)], axis=1)) for h in heads]

    base = 8
    inner = same_block(base)
    p = [bf(jnp.where(inner, m[h], 0.0)) for h in heads]
    inv = [eye - jnp.where(inner, m[h], 0.0) for h in heads]
    size = 2
    while size < base:
        p = [bf(_dot(p[h], p[h])) for h in heads]
        inv = [inv[h] + _dot(bf(inv[h]), p[h]) for h in heads]
        size *= 2
    size = base
    while size < C:
        outer = same_block(2 * size)
        off = [bf(jnp.where(outer & jnp.logical_not(inner), m[h], 0.0)) for h in heads]
        invb = [bf(inv[h]) for h in heads]
        half = [bf(_dot(invb[h], off[h])) for h in heads]
        inv = [inv[h] - _dot(half[h], invb[h]) for h in heads]
        inner = outer
        size *= 2
    x = [_dot(bf(inv[h]), rhs[h]) for h in heads]

    qg = [qf[h] * col(eg_all, h) for h in heads]
    kd_t = [(kf[h].astype(F32) * col(kdec_all, h)).T for h in heads]
    state = [s_scr[h] for h in heads]
    outs = [[] for _ in heads]
    for c in range(nchunk):
        rows = slice(c * C, (c + 1) * C)
        a = [bf(jnp.concatenate([x[h][rows, HEAD_DIM:], qg[h][rows]], axis=0)) for h in heads]
        r = [_dot(a[h], bf(state[h])) for h in heads]
        v_new = [x[h][rows, :HEAD_DIM] - r[h][:C] for h in heads]
        pieces = lambda vn: ([jnp.zeros((c * C, HEAD_DIM), F32)] if c > 0 else []) + [vn] + (
            [jnp.zeros((T - (c + 1) * C, HEAD_DIM), F32)] if c < nchunk - 1 else [])
        v_pad = [bf(jnp.concatenate(pieces(v_new[h]), axis=0)) for h in heads]
        lb = [bf(jnp.concatenate([attn[h][rows], kd_t[h]], axis=0)) for h in heads]
        r2 = [_dot(lb[h], v_pad[h]) for h in heads]
        for h in heads:
            outs[h].append(r[h][C:] + r2[h][:C])
        state = [state[h] * egl_all[c * C:c * C + 1, HEADS + h:HEADS + h + 1] + r2[h][C:] for h in heads]
    for h in heads:
        hs = slice(h * HEAD_DIM, (h + 1) * HEAD_DIM)
        s_scr[h] = state[h]
        o = jnp.concatenate(outs[h], axis=0)
        zf = z_ref[0, :, hs].astype(F32)
        o_ref[0, :, hs] = bf(_rms(o) * ng_ref[...] * _silu(zf))


def _delta_call(qkv, proj, small, l, alog_row, dt_row, gdn_g, z_blk):
    B, S, W = qkv.shape
    dm = W // 3
    T = min(DELTA_T, S)
    return pl.pallas_call(
        _delta_kernel,
        grid=(B, S // T),
        in_specs=[
            pl.BlockSpec((1, T, W), lambda b, t: (b, t, 0)),
            pl.BlockSpec((1, T, dm), lambda b, t: (b, t, z_blk)),
            pl.BlockSpec((1, T, SMALL), lambda b, t: (b, t, 0)),
            pl.BlockSpec((None, 1, SMALL), lambda b, t: (l, 0, 0)),
            pl.BlockSpec((None, 1, SMALL), lambda b, t: (l, 0, 0)),
            pl.BlockSpec((None, 1, HEAD_DIM), lambda b, t: (l, 0, 0)),
        ],
        out_specs=pl.BlockSpec((1, T, dm), lambda b, t: (b, t, 0)),
        out_shape=jax.ShapeDtypeStruct((B, S, dm), BF16),
        scratch_shapes=[pltpu.VMEM((HEADS, HEAD_DIM, HEAD_DIM), F32)],
        compiler_params=_params("parallel", "arbitrary"),
        name="delta",
    )(qkv, proj, small, alog_row, dt_row, gdn_g)


def _mix_kernel(h_ref, gates_ref, xa_ref, ba_ref, ca_ref, xah_ref, cah_ref, uc_ref, vc_ref, yb_ref,
                gt_ref, ng_ref, cw_ref, lng_ref, lnb_ref, wsp_ref, bsp_ref, wb_ref, wo_ref,
                o_ref, ext_scr, yc_scr):
    t = pl.program_id(1)
    T = h_ref.shape[1]
    D = h_ref.shape[2]
    SC = SPATIAL_CHUNK
    nchunk = T // SC

    p = ca_ref[0].astype(F32) * xa_ref[0].astype(F32)
    ph = jnp.where(t > 0, cah_ref[0].astype(F32) * xah_ref[0].astype(F32), 0.0)
    ext_scr[0:HALO, :] = ph
    ext_scr[HALO:HALO + T, :] = p
    K = cw_ref.shape[0]
    conv = cw_ref[K - 1:K, :] * p
    for s in range(1, K):
        conv = conv + cw_ref[K - 1 - s:K - s, :] * ext_scr[HALO - s:HALO - s + T, :]
    y_a = (ba_ref[0].astype(F32) * conv).astype(BF16)

    v = _gelu_tanh(vc_ref[0].astype(F32))
    vcen = v - jnp.mean(v, axis=-1, keepdims=True)
    vn = vcen * lax.rsqrt(jnp.mean(vcen * vcen, axis=-1, keepdims=True) + EPS)
    vn = (vn * lng_ref[...] + lnb_ref[...]).astype(BF16)
    ri = lax.broadcasted_iota(jnp.int32, (SC, SC), 0)
    ci = lax.broadcasted_iota(jnp.int32, (SC, SC), 1)
    causal = ri >= ci
    gw = SC
    for g in range(GROUPS):
        gs = slice(g * gw, (g + 1) * gw)
        wg = jnp.where(causal, wsp_ref[g], 0.0).astype(BF16)
        rhs = jnp.concatenate([vn[c * SC:(c + 1) * SC, gs] for c in range(nchunk)], axis=1)
        mx = _dot(wg, rhs)
        mixed = jnp.concatenate([mx[:, c * gw:(c + 1) * gw] for c in range(nchunk)], axis=0)
        bias = jnp.concatenate([bsp_ref[:, gs]] * nchunk, axis=0)
        u = _gelu_tanh(uc_ref[0, :, gs].astype(F32))
        yc_scr[:, gs] = (u * (mixed + bias)).astype(BF16)

    gates = gates_ref[0]
    merged = _sigmoid(gates[:, 0:D].astype(F32)) * _dot(y_a, wb_ref[0])
    merged = merged + _sigmoid(gates[:, D:2 * D].astype(F32)) * _dot(yb_ref[0], wb_ref[1])
    merged = merged + _sigmoid(gates[:, 2 * D:3 * D].astype(F32)) * _dot(yc_scr[...], wb_ref[2])
    y = _dot(merged.astype(BF16), wo_ref[...])
    o_ref[0] = h_ref[0] + gt_ref[0] * (_rms(y) * ng_ref[0])


def _mix_call(h, proj, yb, ada, ng, l, conv_a, ln_g, ln_b, w_sp, b_sp, wb, wo):
    B, S, D = h.shape
    dm = D // 2
    lay = _proj_layout(D)
    T = min(MIX_T, S)
    rb = T // HALO
    K = conv_a.shape[1]
    one = pl.Buffered(1)
    tile = lambda blk: pl.BlockSpec((1, T, dm), lambda b, t: (b, t, blk))
    halo = lambda blk: pl.BlockSpec((1, HALO, dm), lambda b, t: (b, jnp.maximum(t * rb - 1, 0), blk))
    const = lambda shape: pl.BlockSpec((None,) + shape, lambda b, t: (l,) + (0,) * len(shape), pipeline_mode=one)
    return pl.pallas_call(
        _mix_kernel,
        grid=(B, S // T),
        in_specs=[
            pl.BlockSpec((1, T, D), lambda b, t: (b, t, 0)),
            pl.BlockSpec((1, T, 3 * D), lambda b, t: (b, t, 0)),
            tile(lay["xa"]), tile(lay["ba"]), tile(lay["ca"]),
            halo(lay["xa"]), halo(lay["ca"]),
            tile(lay["uc"]), tile(lay["vc"]),
            pl.BlockSpec((1, T, dm), lambda b, t: (b, t, 0)),
            pl.BlockSpec((1, 1, D), lambda b, t: ((l * B + b) * N_ADA + 5, 0, 0)),
            pl.BlockSpec((1, 1, D), lambda b, t: (l * N_NORMS + 3, 0, 0)),
            const((K, dm)), const((1, dm)), const((1, dm)),
            const((GROUPS, SPATIAL_CHUNK, SPATIAL_CHUNK)), const((SPATIAL_CHUNK, dm)),
            const((3, dm, D)), const((D, D)),
        ],
        out_specs=pl.BlockSpec((1, T, D), lambda b, t: (b, t, 0)),
        out_shape=jax.ShapeDtypeStruct((B, S, D), F32),
        scratch_shapes=[pltpu.VMEM((HALO + T, dm), F32), pltpu.VMEM((T, dm), BF16)],
        compiler_params=_params("parallel", "parallel"),
        name="mix",
    )(h, proj, proj, proj, proj, proj, proj, proj, proj, yb, ada, ng,
      conv_a, ln_g, ln_b, w_sp, b_sp, wb, wo)


def _pad_axis(w, axis, mult):
    pad = (-w.shape[axis]) % mult
    if not pad:
        return w
    zeros = jnp.zeros(w.shape[:axis] + (pad,) + w.shape[axis + 1:], w.dtype)
    return jnp.concatenate([w, zeros], axis=axis)


def kernel(x, c, w_ada, b_ada, norm_g, ffn_w_gate, ffn_w_up, ffn_w_down, w_in, conv_a, conv_qkv, a_log,
           dt_bias, gdn_norm_g, ln_v_g, ln_v_b, w_spatial, b_spatial, w_branch, w_o):
    B, S, D = x.shape
    L = w_ada.shape[0]
    dm = D // 2
    assert dm == HEADS * HEAD_DIM and dm == GROUPS * SPATIAL_CHUNK
    lay = _proj_layout(D)

    ada = _ada_call(c, w_ada, b_ada).reshape(L * B * N_ADA, 1, D)
    ng = norm_g.reshape(L * N_NORMS, 1, D)

    wg = _pad_axis(ffn_w_gate.astype(BF16), 3, FFN_TF)
    wu = _pad_axis(ffn_w_up.astype(BF16), 3, FFN_TF)
    wd = _pad_axis(ffn_w_down.astype(BF16), 2, FFN_TF)
    o_qkv, o_z, o_small = 3 * dm, 6 * dm, 7 * dm
    o_uc = o_small + 2 * HEADS
    o_g = o_uc + 2 * dm
    w_main = jnp.concatenate([w_in[..., o_g:], w_in[..., o_qkv:o_z], w_in[..., :o_qkv], w_in[..., o_z:o_small],
                              w_in[..., o_uc:o_g]], axis=-1).astype(BF16)
    w_small = _pad_axis(w_in[..., o_small:o_uc], 2, SMALL).astype(BF16)
    wb = w_branch.astype(BF16)
    wo = w_o.astype(BF16)
    lane_pad = lambda v: jnp.pad(v, ((0, 0), (HEADS, SMALL - 2 * HEADS))).reshape(L, 1, SMALL)
    alog_rows, dt_rows = lane_pad(a_log), lane_pad(dt_bias)
    gdn_rows = gdn_norm_g.reshape(L, 1, HEAD_DIM)
    ln_g, ln_b = ln_v_g.reshape(L, 1, dm), ln_v_b.reshape(L, 1, dm)
    b_exp = jnp.repeat(jnp.swapaxes(b_spatial, 1, 2), dm // GROUPS, axis=2)

    h = x
    for l in range(L):
        h = _ffn_call(h, ada, ng, l, 0, wg, wu, wd)
        proj, small = _proj_call(h, ada, ng, l, w_main, w_small)
        qkv = _prep_call(proj, l, conv_qkv)
        yb = _delta_call(qkv, proj, small, l, alog_rows, dt_rows, gdn_rows, lay["z"])
        h = _mix_call(h, proj, yb, ada, ng, l, conv_a, ln_g, ln_b, w_spatial, b_exp, wb, wo)
        h = _ffn_call(h, ada, ng, l, 1, wg, wu, wd)
    return h
```
